```python
import math
import jax, jax.numpy as jnp
from jax import lax
import numpy as np

D_MODEL = 1024
BATCH = 8
SEQ = 2048
DEPTH = 2
DEC_BATCH = 16
DEC_SEQ = 4096
PAST_LEN = 128

H_A = 8
HD_QK = 64
HD_V = 128
W_A = H_A * HD_V
Q_BLOCK = 128
ROPE_THETA = 10000.0
SUBLN_EPS = 1e-5
H_B = 8
HD_K = 128
HD_VB = 128
W_B = H_B * HD_VB
CONV_W = 4
CHUNK = 64
EPS = 1e-6
SPLIT_SIZES = (
    H_A * 2 * HD_QK,
    H_A * 2 * HD_QK,
    W_A,
    W_A,
    2 * H_B * HD_K + W_B,
    W_B,
    2 * H_B,
    2 * H_B,
    2 * D_MODEL,
)
IN_WIDTH = sum(SPLIT_SIZES)
SPLIT_POINTS = [int(v) for v in np.cumsum(SPLIT_SIZES)[:-1]]

kernel_name = 'hybrid_diffattn_gdn_encoder'


def rmsnorm(x, g, eps=EPS):
    xf = x.astype(jnp.float32)
    y = xf * lax.rsqrt(jnp.mean(xf * xf, -1, keepdims=True) + eps)
    return (y * g.astype(jnp.float32)).astype(x.dtype)


def l2norm(x):
    return x * lax.rsqrt(jnp.sum(x * x, -1, keepdims=True) + EPS)


def rope_tables(S, dtype):
    inv = ROPE_THETA ** (-jnp.arange(0, HD_QK, 2, dtype=jnp.float32) / HD_QK)
    ang = jnp.arange(S, dtype=jnp.float32)[:, None] * inv[None, :]
    ang = jnp.concatenate([ang, ang], -1)
    return jnp.cos(ang).astype(dtype), jnp.sin(ang).astype(dtype)


def apply_rope(x, cos, sin):
    c = cos[:, None, None, :]
    s = sin[:, None, None, :]
    x1, x2 = jnp.split(x, 2, -1)
    return x * c + jnp.concatenate([-x2, x1], -1) * s


def diff_attention(q, k, v, lam):
    B, H, _, S, Dq = q.shape
    nb = S // Q_BLOCK
    qb = jnp.moveaxis(q.reshape(B, H, 2, nb, Q_BLOCK, Dq), 3, 0)
    scale = Dq ** -0.5

    def one(qi):
        s = jnp.einsum('bhcqd,bhckd->bhcqk', qi, k, preferred_element_type=jnp.float32) * scale
        p = jax.nn.softmax(s, axis=-1)
        w = p[:, :, 0] - lam * p[:, :, 1]
        return jnp.einsum('bhqk,bhkv->bhqv', w.astype(v.dtype), v)

    o = lax.map(one, qb)
    return o.transpose(1, 0, 3, 2, 4).reshape(B, S, H, v.shape[-1])


def diff_branch(q, k, v, z, lam_qk, gain, lam_init, cos, sin):
    B, S, _ = q.shape
    q = apply_rope(q.reshape(B, S, H_A, 2, HD_QK), cos, sin).transpose(0, 2, 3, 1, 4)
    k = apply_rope(k.reshape(B, S, H_A, 2, HD_QK), cos, sin).transpose(0, 2, 3, 1, 4)
    v = v.reshape(B, S, H_A, HD_V).transpose(0, 2, 1, 3)
    lq = lam_qk.astype(jnp.float32)
    lam = jnp.exp(jnp.sum(lq[0] * lq[1])) - jnp.exp(jnp.sum(lq[2] * lq[3])) + lam_init
    o = diff_attention(q, k, v, lam)
    o = rmsnorm(o, gain, SUBLN_EPS) * (1.0 - lam_init)
    return o.reshape(B, S, W_A) * jax.nn.silu(z)


def gdn_direction(q, k, v, g, beta):
    B, H, L, DK = q.shape
    DV = v.shape[-1]
    N = L // CHUNK
    q = q.reshape(B, H, N, CHUNK, DK)
    k = k.reshape(B, H, N, CHUNK, DK)
    v = v.reshape(B, H, N, CHUNK, DV)
    g = g.reshape(B, H, N, CHUNK)
    beta = beta.reshape(B, H, N, CHUNK)
    gc = jnp.cumsum(g, -1)
    idx = jnp.arange(CHUNK)
    tril = idx[:, None] >= idx[None, :]
    strict = idx[:, None] > idx[None, :]
    decay = jnp.exp(jnp.where(tril, gc[..., :, None] - gc[..., None, :], -jnp.inf))
    kb = k * beta[..., None]
    a = jnp.where(strict, jnp.einsum('bhncd,bhnjd->bhncj', kb, k) * decay, 0.0)
    eye = jnp.eye(CHUNK, dtype=a.dtype)
    t = lax.linalg.triangular_solve(a + eye, jnp.broadcast_to(eye, a.shape), left_side=True, lower=True, unit_diagonal=True)
    u = jnp.einsum('bhncj,bhnjv->bhncv', t, v * beta[..., None])
    w = jnp.einsum('bhncj,bhnjd->bhncd', t, kb * jnp.exp(gc)[..., None])
    attn = jnp.where(tril, jnp.einsum('bhncd,bhnjd->bhncj', q, k) * decay, 0.0)
    qg = q * jnp.exp(gc)[..., None]
    kd = k * jnp.exp(gc[..., -1:] - gc)[..., None]
    gl = jnp.exp(gc[..., -1])
    xs = (jnp.moveaxis(u, 2, 0), jnp.moveaxis(w, 2, 0), jnp.moveaxis(attn, 2, 0),
          jnp.moveaxis(qg, 2, 0), jnp.moveaxis(kd, 2, 0), jnp.moveaxis(gl, 2, 0))

    def step(state, inp):
        u_i, w_i, attn_i, qg_i, kd_i, gl_i = inp
        v_new = u_i - jnp.einsum('bhcd,bhdv->bhcv', w_i, state)
        o = jnp.einsum('bhcd,bhdv->bhcv', qg_i, state) + jnp.einsum('bhcj,bhjv->bhcv', attn_i, v_new)
        state = state * gl_i[..., None, None] + jnp.einsum('bhcd,bhcv->bhdv', kd_i, v_new)
        return state, o

    s0 = jnp.zeros((B, H, DK, DV), q.dtype)
    _, o = lax.scan(step, s0, xs)
    return jnp.moveaxis(o, 0, 2).reshape(B, H, L, DV)


def gdn_branch(qkv, z, a, b, conv_w, a_log, dt_bias, gain):
    B, S, C = qkv.shape
    left = (CONV_W - 1) // 2
    qkv = lax.conv_general_dilated(qkv, conv_w[:, None, :].astype(qkv.dtype), (1,), [(left, CONV_W - 1 - left)],
                                   dimension_numbers=('NWC', 'WIO', 'NWC'), feature_group_count=C)
    qkv = jax.nn.silu(qkv).astype(jnp.float32)
    q, k, v = jnp.split(qkv, [H_B * HD_K, 2 * H_B * HD_K], -1)
    q = l2norm(q.reshape(B, S, H_B, HD_K).transpose(0, 2, 1, 3)) * (HD_K ** -0.5)
    k = l2norm(k.reshape(B, S, H_B, HD_K).transpose(0, 2, 1, 3))
    v = v.reshape(B, S, H_B, HD_VB).transpose(0, 2, 1, 3)
    a = a.astype(jnp.float32).reshape(B, S, 2, H_B).transpose(2, 0, 3, 1)
    b = b.astype(jnp.float32).reshape(B, S, 2, H_B).transpose(2, 0, 3, 1)
    g = -jnp.exp(a_log.astype(jnp.float32))[:, None, :, None] * jax.nn.softplus(a + dt_bias.astype(jnp.float32)[:, None, :, None])
    beta = jax.nn.sigmoid(b)
    o_f = gdn_direction(q, k, v, g[0], beta[0])
    fl = lambda t: jnp.flip(t, 2)
    o_b = fl(gdn_direction(fl(q), fl(k), fl(v), fl(g[1]), fl(beta[1])))
    o = (o_f + o_b).transpose(0, 2, 1, 3).astype(z.dtype)
    o = rmsnorm(o, gain)
    return o.reshape(B, S, W_B) * jax.nn.silu(z)


def trunk(x, norm_g, w_in, conv_w, lam_qk, diff_norm_g, a_log, dt_bias, gdn_norm_g, w_branch, w_out, final_g):
    B, S, D = x.shape
    cos, sin = rope_tables(S, x.dtype)
    for l in range(DEPTH):
        h = rmsnorm(x, norm_g[l])
        proj = h @ w_in[l]
        q_a, k_a, v_a, z_a, qkv_b, z_b, a_b, b_b, gate = jnp.split(proj, SPLIT_POINTS, -1)
        lam_init = 0.8 - 0.6 * math.exp(-0.3 * l)
        y_a = diff_branch(q_a, k_a, v_a, z_a, lam_qk[l], diff_norm_g[l], lam_init, cos, sin)
        y_b = gdn_branch(qkv_b, z_b, a_b, b_b, conv_w[l], a_log[l], dt_bias[l], gdn_norm_g[l])
        gates = jax.nn.sigmoid(gate.astype(jnp.float32)).astype(x.dtype).reshape(B, S, 2, D)
        merged = gates[:, :, 0] * (y_a @ w_branch[l, 0]) + gates[:, :, 1] * (y_b @ w_branch[l, 1])
        x = x + merged @ w_out[l]
    return rmsnorm(x, final_g)


def setup_inputs(seed: int = 0) -> dict:
    key = jax.random.key(seed)
    ks = jax.random.split(key, 16)
    f32 = jnp.float32
    x_prompt = jax.random.normal(ks[0], (BATCH, SEQ, D_MODEL), f32)
    x_sample = jax.random.normal(ks[1], (DEC_BATCH, DEC_SEQ, D_MODEL), f32)
    norm_g = 1.0 + 0.02 * jax.random.normal(ks[2], (DEPTH, D_MODEL), f32)
    w_in = jax.random.normal(ks[3], (DEPTH, D_MODEL, IN_WIDTH), f32) * D_MODEL ** -0.5
    conv_w = jax.random.normal(ks[4], (DEPTH, CONV_W, 2 * H_B * HD_K + W_B), f32) * CONV_W ** -0.5
    lam_qk = 0.1 * jax.random.normal(ks[5], (DEPTH, 4, HD_QK), f32)
    diff_norm_g = 1.0 + 0.02 * jax.random.normal(ks[6], (DEPTH, HD_V), f32)
    a_log = jnp.log(jax.random.uniform(ks[7], (DEPTH, 2, H_B), f32, 1.0, 16.0))
    dt = jnp.exp(jax.random.uniform(ks[8], (DEPTH, 2, H_B), f32, math.log(0.001), math.log(0.1)))
    dt_bias = dt + jnp.log(-jnp.expm1(-dt))
    gdn_norm_g = 1.0 + 0.02 * jax.random.normal(ks[9], (DEPTH, HD_VB), f32)
    w_branch = jax.random.normal(ks[10], (DEPTH, 2, W_A, D_MODEL), f32) * W_A ** -0.5
    w_out = jax.random.normal(ks[11], (DEPTH, D_MODEL, D_MODEL), f32) * D_MODEL ** -0.5
    final_g = 1.0 + 0.02 * jax.random.normal(ks[12], (D_MODEL,), f32)
    return {'x_prompt': x_prompt, 'x_sample': x_sample, 'norm_g': norm_g, 'w_in': w_in, 'conv_w': conv_w,
            'lam_qk': lam_qk, 'diff_norm_g': diff_norm_g, 'a_log': a_log, 'dt_bias': dt_bias,
            'gdn_norm_g': gdn_norm_g, 'w_branch': w_branch, 'w_out': w_out, 'final_g': final_g}


def reference(x_prompt, x_sample, norm_g, w_in, conv_w, lam_qk, diff_norm_g, a_log, dt_bias, gdn_norm_g, w_branch, w_out, final_g):
    y_prompt = trunk(x_prompt, norm_g, w_in, conv_w, lam_qk, diff_norm_g, a_log, dt_bias, gdn_norm_g, w_branch, w_out, final_g)
    y_sample = trunk(x_sample, norm_g, w_in, conv_w, lam_qk, diff_norm_g, a_log, dt_bias, gdn_norm_g, w_branch, w_out, final_g)
    return (y_prompt, y_sample)
```

```python
import functools
import math

import jax
import jax.numpy as jnp
from jax import lax
from jax.experimental import pallas as pl
from jax.experimental.pallas import tpu as pltpu

F32 = jnp.float32
BF16 = jnp.bfloat16

LANES = 128
N_HEADS = 8
HD_QK = 64
HEAD_W = 128
CONV_W = 4
CONV_LEFT = (CONV_W - 1) // 2
CHUNK = 64
ROPE_THETA = 10000.0
NORM_EPS = 1e-6
SUBLN_EPS = 1e-5
VMEM_LIMIT = 56 * 1024 * 1024


def _cparams(sem):
    return pltpu.CompilerParams(dimension_semantics=sem, vmem_limit_bytes=VMEM_LIMIT)


def _silu(x):
    return x * (1.0 / (1.0 + jnp.exp(-x)))


def _sigmoid(x):
    return 1.0 / (1.0 + jnp.exp(-x))


def _rmsnorm_kernel(x_ref, g_ref, o_ref, *, eps):
    x = x_ref[...]
    y = x * lax.rsqrt(jnp.mean(x * x, axis=-1, keepdims=True) + eps)
    o_ref[...] = (y * g_ref[...]).astype(o_ref.dtype)


def _rmsnorm(x, g, out_dtype, tm=1024):
    m, d = x.shape
    tm = min(tm, m)
    return pl.pallas_call(
        functools.partial(_rmsnorm_kernel, eps=NORM_EPS),
        grid=(m // tm,),
        in_specs=[pl.BlockSpec((tm, d), lambda i: (i, 0)), pl.BlockSpec((1, d), lambda i: (0, 0))],
        out_specs=pl.BlockSpec((tm, d), lambda i: (i, 0)),
        out_shape=jax.ShapeDtypeStruct((m, d), out_dtype),
        compiler_params=_cparams(("parallel",)),
        name="rmsnorm",
    )(x, g.reshape(1, d))


def _proj_plain_kernel(h_ref, w_ref, o_ref):
    o_ref[...] = jnp.dot(h_ref[...], w_ref[...], preferred_element_type=F32).astype(o_ref.dtype)


def _proj_sigmoid_kernel(h_ref, w_ref, o_ref):
    acc = jnp.dot(h_ref[...], w_ref[...], preferred_element_type=F32)
    o_ref[...] = _sigmoid(acc).astype(o_ref.dtype)


def _proj_rope_kernel(h_ref, w_ref, cos_ref, sin_ref, o_ref):
    acc = jnp.dot(h_ref[...], w_ref[...], preferred_element_type=F32)
    cos = cos_ref[...]
    sin = sin_ref[...]
    lane = lax.broadcasted_iota(jnp.int32, cos.shape, 1)
    first_half = (lane % HD_QK) < (HD_QK // 2)
    for g in range(acc.shape[1] // LANES):
        a = acc[:, g * LANES:(g + 1) * LANES]
        rot = jnp.where(first_half, pltpu.roll(a, LANES - HD_QK // 2, 1), pltpu.roll(a, HD_QK // 2, 1))
        o_ref[:, g * LANES:(g + 1) * LANES] = (a * cos + rot * sin).astype(o_ref.dtype)


def _proj_gdn_gate_kernel(h_ref, w_ref, alog_ref, dtb_ref, o_ref):
    acc = jnp.dot(h_ref[...], w_ref[...], preferred_element_type=F32)
    lane = lax.broadcasted_iota(jnp.int32, acc.shape, 1)
    x = acc + dtb_ref[...]
    softplus = jnp.maximum(x, 0.0) + jnp.log(1.0 + jnp.exp(-jnp.abs(x)))
    g = -jnp.exp(alog_ref[...]) * softplus
    o_ref[...] = jnp.where(lane < 2 * N_HEADS, g, _sigmoid(acc))


def _proj(kernel_fn, h, w, extras, extra_specs, out_dtype, tm, tn, name):
    m, d = h.shape
    n = w.shape[1]
    return pl.pallas_call(
        kernel_fn,
        grid=(m // tm, n // tn),
        in_specs=[pl.BlockSpec((tm, d), lambda i, j: (i, 0)), pl.BlockSpec((d, tn), lambda i, j: (0, j))]
        + extra_specs,
        out_specs=pl.BlockSpec((tm, tn), lambda i, j: (i, j)),
        out_shape=jax.ShapeDtypeStruct((m, n), out_dtype),
        compiler_params=_cparams(("parallel", "arbitrary")),
        name=name,
    )(h, w, *extras)


def _attn_kernel(lam_ref, q_ref, k_ref, v_ref, z_ref, g_ref, o_ref, *, tk, post_scale):
    tq = q_ref.shape[1]
    s_len = k_ref.shape[1]
    q = q_ref[0]
    lane = lax.broadcasted_iota(jnp.int32, q.shape, 1)
    zero = jnp.zeros_like(q)
    q2 = jnp.concatenate([jnp.where(lane < HD_QK, q, zero), jnp.where(lane >= HD_QK, q, zero)], axis=0)

    def body(kt, carry):
        m, l, acc = carry
        start = pl.multiple_of(kt * tk, tk)
        k = k_ref[0, pl.ds(start, tk), :]
        v = v_ref[0, pl.ds(start, tk), :]
        s = lax.dot_general(q2, k, (((1,), (1,)), ((), ())), preferred_element_type=F32)
        m_new = jnp.maximum(m, jnp.max(s, axis=1, keepdims=True))
        alpha = jnp.exp(m - m_new)
        p = jnp.exp(s - m_new)
        l = alpha * l + jnp.sum(p, axis=1, keepdims=True)
        acc = alpha * acc + jnp.dot(p.astype(BF16), v, preferred_element_type=F32)
        return m_new, l, acc

    init = (jnp.full((2 * tq, 1), -jnp.inf, F32), jnp.zeros((2 * tq, 1), F32), jnp.zeros((2 * tq, HEAD_W), F32))
    _, l, acc = lax.fori_loop(0, s_len // tk, body, init)
    o2 = acc / l
    o = o2[:tq] - lam_ref[0] * o2[tq:]
    y = o * lax.rsqrt(jnp.mean(o * o, axis=-1, keepdims=True) + SUBLN_EPS) * g_ref[...] * post_scale
    o_ref[0] = (y * _silu(z_ref[0])).astype(o_ref.dtype)


def _diff_attention(lam, qk, v, z, gain, post_scale, tq=256, tk=512):
    b, s, w = v.shape
    nh = w // HEAD_W
    tq = min(tq, s)
    tk = min(tk, s)
    return pl.pallas_call(
        functools.partial(_attn_kernel, tk=tk, post_scale=post_scale),
        grid=(b, nh, s // tq),
        in_specs=[
            pl.BlockSpec(memory_space=pltpu.SMEM),
            pl.BlockSpec((1, tq, HEAD_W), lambda bi, hi, qi: (bi, qi, hi)),
            pl.BlockSpec((1, s, HEAD_W), lambda bi, hi, qi: (bi, 0, nh + hi)),
            pl.BlockSpec((1, s, HEAD_W), lambda bi, hi, qi: (bi, 0, hi)),
            pl.BlockSpec((1, tq, HEAD_W), lambda bi, hi, qi: (bi, qi, hi)),
            pl.BlockSpec((1, HEAD_W), lambda bi, hi, qi: (0, 0)),
        ],
        out_specs=pl.BlockSpec((1, tq, HEAD_W), lambda bi, hi, qi: (bi, qi, hi)),
        out_shape=jax.ShapeDtypeStruct((b, s, w), BF16),
        compiler_params=_cparams(("parallel", "parallel", "arbitrary")),
        name="diff_attention",
    )(lam, qk, qk, v, z, gain.reshape(1, HEAD_W))


def _dot(a, b):
    return jnp.dot(a.astype(BF16), b.astype(BF16), preferred_element_type=F32)


def _dot_nt(a, b):
    return lax.dot_general(a.astype(BF16), b.astype(BF16), (((1,), (1,)), ((), ())), preferred_element_type=F32)


def _dot_tn(a, b):
    return lax.dot_general(a.astype(BF16), b.astype(BF16), (((0,), (0,)), ((), ())), preferred_element_type=F32)


def _dot_exact(a, b):
    return jnp.dot(a, b, preferred_element_type=F32, precision=lax.Precision.HIGHEST)


def _unit_triangular_inverse(a):
    n = a.shape[0]
    row = lax.broadcasted_iota(jnp.int32, a.shape, 0)
    col = lax.broadcasted_iota(jnp.int32, a.shape, 1)
    t = jnp.where(row == col, 1.0, 0.0).astype(F32) - a
    power = a
    for _ in range(int(math.log2(n)) - 1):
        power = _dot(power, power)
        t = t + _dot(t, power)
    return t


def _conv_silu(x_ref, w_ref, blk, rows, n_blk):
    t0 = pl.multiple_of(blk * rows, rows)
    xc = x_ref[0, pl.ds(t0, rows), :].astype(F32)
    prev_start = pl.multiple_of(jnp.maximum(t0 - 8, 0), 8)
    next_start = pl.multiple_of(jnp.minimum(t0 + rows, (n_blk - 1) * rows + rows - 8), 8)
    prev8 = x_ref[0, pl.ds(prev_start, 8), :].astype(F32)
    next8 = x_ref[0, pl.ds(next_start, 8), :].astype(F32)
    prev8 = jnp.where(blk == 0, 0.0, prev8)
    next8 = jnp.where(blk == n_blk - 1, 0.0, next8)
    r = lax.broadcasted_iota(jnp.int32, xc.shape, 0)
    x_m1 = jnp.where(r == 0, prev8[7:8, :], pltpu.roll(xc, 1, 0))
    x_p1 = jnp.where(r == rows - 1, next8[0:1, :], pltpu.roll(xc, rows - 1, 0))
    x_p2 = jnp.where(r == rows - 2, next8[0:1, :], jnp.where(r == rows - 1, next8[1:2, :], pltpu.roll(xc, rows - 2, 0)))
    w = w_ref[...]
    y = x_m1 * w[0:1, :] + xc * w[1:2, :] + x_p1 * w[2:3, :] + x_p2 * w[3:4, :]
    return _silu(y)


def _gdn_kernel(q_ref, k_ref, v_ref, gate_ref, z_ref, wq_ref, wk_ref, wv_ref, gain_ref, o_ref,
                qs_ref, ks_ref, vs_ref, acc_ref, *, conv_rows):
    s_len = q_ref.shape[1]
    n_chunks = s_len // CHUNK
    head = pl.program_id(1)

    n_blk = s_len // conv_rows

    def conv_body(blk, _):
        t0 = pl.multiple_of(blk * conv_rows, conv_rows)
        q = _conv_silu(q_ref, wq_ref, blk, conv_rows, n_blk)
        k = _conv_silu(k_ref, wk_ref, blk, conv_rows, n_blk)
        v = _conv_silu(v_ref, wv_ref, blk, conv_rows, n_blk)
        q = q * lax.rsqrt(jnp.sum(q * q, axis=-1, keepdims=True) + NORM_EPS) * (HEAD_W ** -0.5)
        k = k * lax.rsqrt(jnp.sum(k * k, axis=-1, keepdims=True) + NORM_EPS)
        qs_ref[pl.ds(t0, conv_rows), :] = q
        ks_ref[pl.ds(t0, conv_rows), :] = k
        vs_ref[pl.ds(t0, conv_rows), :] = v
        return 0

    lax.fori_loop(0, n_blk, conv_body, 0)

    row = lax.broadcasted_iota(jnp.int32, (CHUNK, CHUNK), 0)
    col = lax.broadcasted_iota(jnp.int32, (CHUNK, CHUNK), 1)
    lane = lax.broadcasted_iota(jnp.int32, (CHUNK, LANES), 1)

    def chunk_step(n, state, reverse):
        t0 = pl.multiple_of(n * CHUNK, CHUNK)
        q = qs_ref[pl.ds(t0, CHUNK), :]
        k = ks_ref[pl.ds(t0, CHUNK), :]
        v = vs_ref[pl.ds(t0, CHUNK), :]
        gates = gate_ref[0, pl.ds(t0, CHUNK), :]
        d = N_HEADS if reverse else 0
        g = jnp.sum(jnp.where(lane == head + d, gates, 0.0), axis=1, keepdims=True)
        beta = jnp.sum(jnp.where(lane == head + 2 * N_HEADS + d, gates, 0.0), axis=1, keepdims=True)
        if reverse:
            incl, strict, last = row <= col, row < col, 0
        else:
            incl, strict, last = row >= col, row > col, CHUNK - 1
        ones_incl = jnp.where(incl, 1.0, 0.0).astype(F32)
        g_b = jnp.broadcast_to(g, (CHUNK, LANES))
        gc = _dot_exact(ones_incl, g_b)
        gdiff = _dot_exact(ones_incl, jnp.where(strict, g_b[:, :CHUNK], 0.0))
        decay = jnp.exp(gdiff)
        egc = jnp.exp(gc)
        gl = gc[last:last + 1, :]
        kb = k * beta
        a = jnp.where(strict, _dot_nt(kb, k) * decay, 0.0)
        t = _unit_triangular_inverse(a)
        u = _dot(t, v * beta)
        w = _dot(t, kb * egc)
        attn = jnp.where(incl, _dot_nt(q, k) * decay, 0.0)
        qg = q * egc
        kd = k * jnp.exp(gl - gc)
        v_new = u - _dot(w, state)
        o = _dot(qg, state) + _dot(attn, v_new)
        state = state * jnp.exp(gl) + _dot_tn(kd, v_new)
        return o, state

    state0 = jnp.zeros((HEAD_W, HEAD_W), F32)

    def fwd_body(n, state):
        o, state = chunk_step(n, state, False)
        acc_ref[pl.ds(pl.multiple_of(n * CHUNK, CHUNK), CHUNK), :] = o
        return state

    lax.fori_loop(0, n_chunks, fwd_body, state0)

    def bwd_body(i, state):
        n = n_chunks - 1 - i
        t0 = pl.multiple_of(n * CHUNK, CHUNK)
        o_b, state = chunk_step(n, state, True)
        o = acc_ref[pl.ds(t0, CHUNK), :] + o_b
        y = o * lax.rsqrt(jnp.mean(o * o, axis=-1, keepdims=True) + NORM_EPS) * gain_ref[...]
        o_ref[0, pl.ds(t0, CHUNK), :] = (y * _silu(z_ref[0, pl.ds(t0, CHUNK), :])).astype(o_ref.dtype)
        return state

    lax.fori_loop(0, n_chunks, bwd_body, state0)


def _gdn(qkv, gates, z, conv_w, gain):
    b, s, _ = qkv.shape
    nh = N_HEADS
    conv_rows = min(256, s)
    seq_spec = lambda off: pl.BlockSpec((1, s, HEAD_W), lambda bi, hi: (bi, 0, hi + off))
    w_spec = lambda off: pl.BlockSpec((CONV_W, HEAD_W), lambda bi, hi: (0, hi + off))
    return pl.pallas_call(
        functools.partial(_gdn_kernel, conv_rows=conv_rows),
        grid=(b, nh),
        in_specs=[
            seq_spec(0), seq_spec(nh), seq_spec(2 * nh),
            pl.BlockSpec((1, s, LANES), lambda bi, hi: (bi, 0, 0)),
            seq_spec(nh),
            w_spec(0), w_spec(nh), w_spec(2 * nh),
            pl.BlockSpec((1, HEAD_W), lambda bi, hi: (0, 0)),
        ],
        out_specs=seq_spec(0),
        out_shape=jax.ShapeDtypeStruct((b, s, nh * HEAD_W), BF16),
        scratch_shapes=[pltpu.VMEM((s, HEAD_W), F32) for _ in range(4)],
        compiler_params=_cparams(("parallel", "arbitrary")),
        name="gated_delta",
    )(qkv, qkv, qkv, gates, z, conv_w, conv_w, conv_w, gain.reshape(1, HEAD_W))


def _merge_kernel(ya_ref, yb_ref, gate_ref, x_ref, wa_ref, wb_ref, wo_ref, g_ref, *out_refs, final, tn):
    d = x_ref.shape[1]
    merged = []
    for j in range(d // tn):
        sl = slice(j * tn, (j + 1) * tn)
        pa = jnp.dot(ya_ref[...], wa_ref[:, sl], preferred_element_type=F32)
        pb = jnp.dot(yb_ref[...], wb_ref[:, sl], preferred_element_type=F32)
        ga = gate_ref[:, j * tn:(j + 1) * tn]
        gb = gate_ref[:, d + j * tn:d + (j + 1) * tn]
        merged.append((ga * pa + gb * pb).astype(BF16))
    merged = jnp.concatenate(merged, axis=1)
    x = x_ref[...] + jnp.dot(merged, wo_ref[...], preferred_element_type=F32)
    y = x * lax.rsqrt(jnp.mean(x * x, axis=-1, keepdims=True) + NORM_EPS) * g_ref[...]
    if final:
        out_refs[0][...] = y
    else:
        out_refs[0][...] = x
        out_refs[1][...] = y.astype(BF16)


def _merge(ya, yb, gates, x, wa, wb, wo, g_next, final, tm=512, tn=256):
    m, d = x.shape
    tm = min(tm, m)
    row = lambda width: pl.BlockSpec((tm, width), lambda i: (i, 0))
    full = lambda r, c: pl.BlockSpec((r, c), lambda i: (0, 0))
    if final:
        out_shape = [jax.ShapeDtypeStruct((m, d), F32)]
        out_specs = [row(d)]
    else:
        out_shape = [jax.ShapeDtypeStruct((m, d), F32), jax.ShapeDtypeStruct((m, d), BF16)]
        out_specs = [row(d), row(d)]
    return pl.pallas_call(
        functools.partial(_merge_kernel, final=final, tn=tn),
        grid=(m // tm,),
        in_specs=[row(d), row(d), row(2 * d), row(d), full(d, d), full(d, d), full(d, d), full(1, d)],
        out_specs=out_specs,
        out_shape=out_shape,
        compiler_params=_cparams(("parallel",)),
        name="merge_out",
    )(ya, yb, gates, x, wa, wb, wo, g_next.reshape(1, d))


def _rope_tables(s):
    inv = ROPE_THETA ** (-jnp.arange(0, HD_QK, 2, dtype=F32) / HD_QK)
    ang = jnp.arange(s, dtype=F32)[:, None] * inv[None, :]
    ang = jnp.concatenate([ang, ang], -1)
    sign = jnp.where(jnp.arange(HD_QK) < HD_QK // 2, -1.0, 1.0).astype(F32)
    cos = jnp.tile(jnp.cos(ang), (1, LANES // HD_QK))
    sin = jnp.tile(jnp.sin(ang) * sign[None, :], (1, LANES // HD_QK))
    return cos, sin


def _trunk(x, norm_g, w_in, conv_w, lam_qk, diff_norm_g, a_log, dt_bias, gdn_norm_g, w_branch, w_out, final_g):
    b, s, d = x.shape
    depth = w_in.shape[0]
    m = b * s
    w_a = N_HEADS * HEAD_W
    qk_w = N_HEADS * 2 * HD_QK
    c_q, c_k, c_v, c_za = 0, qk_w, 2 * qk_w, 2 * qk_w + w_a
    c_qkvb = c_za + w_a
    c_zb = c_qkvb + 3 * w_a
    c_a = c_zb + w_a
    c_b = c_a + 2 * N_HEADS
    c_gate = c_b + 2 * N_HEADS

    cos, sin = _rope_tables(s)
    tm = min(1024, s)
    pos_blocks = s // tm
    rope_specs = [pl.BlockSpec((tm, LANES), lambda i, j: (i % pos_blocks, 0))] * 2
    vec_spec = [pl.BlockSpec((1, LANES), lambda i, j: (0, 0))] * 2

    x2 = x.reshape(m, d)
    h = _rmsnorm(x2, norm_g[0], BF16)
    for l in range(depth):
        w = w_in[l]
        scale = HD_QK ** -0.5
        w_qk = jnp.concatenate([w[:, c_q:c_k] * scale, w[:, c_k:c_v]], axis=1).astype(BF16)
        w_v = w[:, c_v:c_za].astype(BF16)
        w_z = jnp.concatenate([w[:, c_za:c_qkvb], w[:, c_zb:c_a]], axis=1).astype(BF16)
        w_qkvb = w[:, c_qkvb:c_zb].astype(BF16)
        w_ab = jnp.pad(w[:, c_a:c_gate], ((0, 0), (0, LANES - 4 * N_HEADS))).astype(BF16)
        w_gate = w[:, c_gate:].astype(BF16)
        alog_vec = jnp.pad(a_log[l].reshape(1, -1), ((0, 0), (0, LANES - 2 * N_HEADS)))
        dtb_vec = jnp.pad(dt_bias[l].reshape(1, -1), ((0, 0), (0, LANES - 2 * N_HEADS)))

        qk = _proj(_proj_rope_kernel, h, w_qk, (cos, sin), rope_specs, BF16, tm, 512, "proj_qk_rope")
        v_a = _proj(_proj_plain_kernel, h, w_v, (), [], BF16, tm, 512, "proj_v")
        z_ab = _proj(_proj_plain_kernel, h, w_z, (), [], F32, tm, 512, "proj_z")
        qkv_b = _proj(_proj_plain_kernel, h, w_qkvb, (), [], F32, tm, 512, "proj_qkv_delta")
        gdn_gates = _proj(_proj_gdn_gate_kernel, h, w_ab, (alog_vec, dtb_vec), vec_spec, F32, tm, LANES, "proj_delta_gates")
        merge_gates = _proj(_proj_sigmoid_kernel, h, w_gate, (), [], F32, tm, 512, "proj_merge_gates")

        lam_init = 0.8 - 0.6 * math.exp(-0.3 * l)
        lq = lam_qk[l].astype(F32)
        lam = (jnp.exp(jnp.sum(lq[0] * lq[1])) - jnp.exp(jnp.sum(lq[2] * lq[3])) + lam_init).reshape(1)

        z_ab = z_ab.reshape(b, s, 2 * w_a)
        y_a = _diff_attention(lam, qk.reshape(b, s, 2 * qk_w), v_a.reshape(b, s, w_a), z_ab,
                              diff_norm_g[l], 1.0 - lam_init)
        y_b = _gdn(qkv_b.reshape(b, s, 3 * w_a), gdn_gates.reshape(b, s, LANES), z_ab, conv_w[l], gdn_norm_g[l])

        final = l == depth - 1
        g_next = final_g if final else norm_g[l + 1]
        outs = _merge(y_a.reshape(m, w_a), y_b.reshape(m, w_a), merge_gates, x2,
                      w_branch[l, 0].astype(BF16), w_branch[l, 1].astype(BF16), w_out[l].astype(BF16),
                      g_next, final)
        if final:
            x2 = outs[0]
        else:
            x2, h = outs
    return x2.reshape(b, s, d)


def kernel(x_prompt, x_sample, norm_g, w_in, conv_w, lam_qk, diff_norm_g, a_log, dt_bias, gdn_norm_g, w_branch, w_out, final_g):
    params = (norm_g, w_in, conv_w, lam_qk, diff_norm_g, a_log, dt_bias, gdn_norm_g, w_branch, w_out, final_g)
    return (_trunk(x_prompt, *params), _trunk(x_sample, *params))
```

```python
import functools
import math

import jax
import jax.numpy as jnp
from jax import lax
from jax.experimental import pallas as pl
from jax.experimental.pallas import tpu as pltpu

F32 = jnp.float32
BF16 = jnp.bfloat16

LANES = 128
N_HEADS = 8
HD_QK = 64
HEAD_W = 128
CONV_W = 4
CHUNK = 128
ROPE_THETA = 10000.0
NORM_EPS = 1e-6
SUBLN_EPS = 1e-5
VMEM_LIMIT = 56 * 1024 * 1024
MASKED = -1e30


def _cparams(sem):
    return pltpu.CompilerParams(dimension_semantics=sem, vmem_limit_bytes=VMEM_LIMIT)


def _silu(x):
    return x * (1.0 / (1.0 + jnp.exp(-x)))


def _sigmoid(x):
    return 1.0 / (1.0 + jnp.exp(-x))


def _rmsnorm_kernel(x_ref, g_ref, o_ref, *, eps):
    x = x_ref[...]
    y = x * lax.rsqrt(jnp.mean(x * x, axis=-1, keepdims=True) + eps)
    o_ref[...] = (y * g_ref[...]).astype(o_ref.dtype)


def _rmsnorm(x, g, out_dtype, tm=1024):
    m, d = x.shape
    tm = min(tm, m)
    return pl.pallas_call(
        functools.partial(_rmsnorm_kernel, eps=NORM_EPS),
        grid=(m // tm,),
        in_specs=[pl.BlockSpec((tm, d), lambda i: (i, 0)), pl.BlockSpec((1, d), lambda i: (0, 0))],
        out_specs=pl.BlockSpec((tm, d), lambda i: (i, 0)),
        out_shape=jax.ShapeDtypeStruct((m, d), out_dtype),
        compiler_params=_cparams(("parallel",)),
        name="rmsnorm",
    )(x, g.reshape(1, d))


def _proj_plain_kernel(h_ref, w_ref, o_ref):
    o_ref[...] = jnp.dot(h_ref[...], w_ref[...], preferred_element_type=F32).astype(o_ref.dtype)


def _proj_sigmoid_kernel(h_ref, w_ref, o_ref):
    acc = jnp.dot(h_ref[...], w_ref[...], preferred_element_type=F32)
    o_ref[...] = _sigmoid(acc).astype(o_ref.dtype)


def _proj_rope_kernel(h_ref, w_ref, cos_ref, sin_ref, o_ref):
    acc = jnp.dot(h_ref[...], w_ref[...], preferred_element_type=F32)
    cos = cos_ref[...]
    sin = sin_ref[...]
    lane = lax.broadcasted_iota(jnp.int32, cos.shape, 1)
    first_half = (lane % HD_QK) < (HD_QK // 2)
    for g in range(acc.shape[1] // LANES):
        a = acc[:, g * LANES:(g + 1) * LANES]
        rot = jnp.where(first_half, pltpu.roll(a, LANES - HD_QK // 2, 1), pltpu.roll(a, HD_QK // 2, 1))
        o_ref[:, g * LANES:(g + 1) * LANES] = (a * cos + rot * sin).astype(o_ref.dtype)


def _proj(kernel_fn, h, w, extras, extra_specs, out_dtype, tm, tn, name):
    m, d = h.shape
    n = w.shape[1]
    return pl.pallas_call(
        kernel_fn,
        grid=(m // tm, n // tn),
        in_specs=[pl.BlockSpec((tm, d), lambda i, j: (i, 0)), pl.BlockSpec((d, tn), lambda i, j: (0, j))]
        + extra_specs,
        out_specs=pl.BlockSpec((tm, tn), lambda i, j: (i, j)),
        out_shape=jax.ShapeDtypeStruct((m, n), out_dtype),
        compiler_params=_cparams(("parallel", "arbitrary")),
        name=name,
    )(h, w, *extras)


def _proj_gdn_gate_kernel(h_ref, w_ref, alog_ref, dtb_ref, col_ref, row_ref):
    acc = jnp.dot(h_ref[...], w_ref[...], preferred_element_type=F32)
    lane = lax.broadcasted_iota(jnp.int32, acc.shape, 1)
    x = acc + dtb_ref[...]
    softplus = jnp.maximum(x, 0.0) + jnp.log(1.0 + jnp.exp(-jnp.abs(x)))
    g = jnp.where(lane < 2 * N_HEADS, -jnp.exp(alog_ref[...]) * softplus, 0.0)
    beta = _sigmoid(acc)
    r = lax.broadcasted_iota(jnp.int32, (CHUNK, CHUNK), 0)
    c = lax.broadcasted_iota(jnp.int32, (CHUNK, CHUNK), 1)
    tril = jnp.where(r >= c, 1.0, 0.0).astype(F32)
    lane_c = lax.broadcasted_iota(jnp.int32, (CHUNK, LANES), 1)
    for ci in range(acc.shape[0] // CHUNK):
        rows = slice(ci * CHUNK, (ci + 1) * CHUNK)
        g_c = g[rows]
        prefix = jnp.dot(tril, g_c, preferred_element_type=F32, precision=lax.Precision.HIGHEST)
        suffix = prefix[CHUNK - 1:CHUNK, :] - prefix + g_c
        out = jnp.where(lane_c < N_HEADS, prefix, jnp.where(lane_c < 2 * N_HEADS, suffix, beta[rows]))
        col_ref[rows, :] = out
        row_ref[:, rows] = out.T[:2 * N_HEADS, :]


def _proj_gdn_gates(h, w, alog_vec, dtb_vec, tm):
    m, d = h.shape
    vec = pl.BlockSpec((1, LANES), lambda i: (0, 0))
    return pl.pallas_call(
        _proj_gdn_gate_kernel,
        grid=(m // tm,),
        in_specs=[pl.BlockSpec((tm, d), lambda i: (i, 0)), pl.BlockSpec((d, LANES), lambda i: (0, 0)), vec, vec],
        out_specs=[pl.BlockSpec((tm, LANES), lambda i: (i, 0)), pl.BlockSpec((2 * N_HEADS, tm), lambda i: (0, i))],
        out_shape=[jax.ShapeDtypeStruct((m, LANES), F32), jax.ShapeDtypeStruct((2 * N_HEADS, m), F32)],
        compiler_params=_cparams(("parallel",)),
        name="proj_delta_gates",
    )(h, w, alog_vec, dtb_vec)


def _attn_kernel(lam_ref, q_ref, k_ref, v_ref, z_ref, g_ref, o_ref, *, tk, post_scale):
    tq = q_ref.shape[1]
    s_len = k_ref.shape[1]
    q = q_ref[0]
    lane = lax.broadcasted_iota(jnp.int32, q.shape, 1)
    zero = jnp.zeros_like(q)
    q2 = jnp.concatenate([jnp.where(lane < HD_QK, q, zero), jnp.where(lane >= HD_QK, q, zero)], axis=0)

    def body(kt, carry):
        m, l, acc = carry
        start = pl.multiple_of(kt * tk, tk)
        k = k_ref[0, pl.ds(start, tk), :]
        v = v_ref[0, pl.ds(start, tk), :]
        s = lax.dot_general(q2, k, (((1,), (1,)), ((), ())), preferred_element_type=F32)
        m_new = jnp.maximum(m, jnp.max(s, axis=1, keepdims=True))
        alpha = jnp.exp(m - m_new)
        p = jnp.exp(s - m_new)
        l = alpha * l + jnp.sum(p, axis=1, keepdims=True)
        acc = alpha * acc + jnp.dot(p.astype(BF16), v, preferred_element_type=F32)
        return m_new, l, acc

    init = (jnp.full((2 * tq, 1), -jnp.inf, F32), jnp.zeros((2 * tq, 1), F32), jnp.zeros((2 * tq, HEAD_W), F32))
    _, l, acc = lax.fori_loop(0, s_len // tk, body, init)
    o2 = acc / l
    o = o2[:tq] - lam_ref[0] * o2[tq:]
    y = o * lax.rsqrt(jnp.mean(o * o, axis=-1, keepdims=True) + SUBLN_EPS) * g_ref[...] * post_scale
    o_ref[0] = (y * _silu(z_ref[0])).astype(o_ref.dtype)


def _diff_attention(lam, qk, v, z, gain, post_scale, tq=256, tk=512):
    b, s, w = v.shape
    nh = w // HEAD_W
    tq = min(tq, s)
    tk = min(tk, s)
    return pl.pallas_call(
        functools.partial(_attn_kernel, tk=tk, post_scale=post_scale),
        grid=(b, nh, s // tq),
        in_specs=[
            pl.BlockSpec(memory_space=pltpu.SMEM),
            pl.BlockSpec((1, tq, HEAD_W), lambda bi, hi, qi: (bi, qi, hi)),
            pl.BlockSpec((1, s, HEAD_W), lambda bi, hi, qi: (bi, 0, nh + hi)),
            pl.BlockSpec((1, s, HEAD_W), lambda bi, hi, qi: (bi, 0, hi)),
            pl.BlockSpec((1, tq, HEAD_W), lambda bi, hi, qi: (bi, qi, hi)),
            pl.BlockSpec((1, HEAD_W), lambda bi, hi, qi: (0, 0)),
        ],
        out_specs=pl.BlockSpec((1, tq, HEAD_W), lambda bi, hi, qi: (bi, qi, hi)),
        out_shape=jax.ShapeDtypeStruct((b, s, w), BF16),
        compiler_params=_cparams(("parallel", "parallel", "arbitrary")),
        name="diff_attention",
    )(lam, qk, qk, v, z, gain.reshape(1, HEAD_W))


def _dot(a, b):
    return jnp.dot(a.astype(BF16), b.astype(BF16), preferred_element_type=F32)


def _dot_nt(a, b):
    return lax.dot_general(a.astype(BF16), b.astype(BF16), (((1,), (1,)), ((), ())), preferred_element_type=F32)


INV_BLOCK = 64
LOCAL_UNROLL = 4


def _unit_triangular_inverses(mats):
    n = mats[0].shape[0]
    row = lax.broadcasted_iota(jnp.int32, (n, n), 0)
    col = lax.broadcasted_iota(jnp.int32, (n, n), 1)
    same_block = (row // INV_BLOCK) == (col // INV_BLOCK)
    eye = jnp.where(row == col, 1.0, 0.0).astype(F32)
    a_diag = [jnp.where(same_block, a, 0.0) for a in mats]
    a_off = [jnp.where(same_block, 0.0, a) for a in mats]
    ts = [eye - a for a in a_diag]
    qs = [_dot(a, a) for a in a_diag]
    levels = int(math.log2(INV_BLOCK)) - 1
    for level in range(levels):
        if level < levels - 1:
            prods = [_dot(q, jnp.concatenate([q, t], axis=1)) for q, t in zip(qs, ts)]
            qs = [p[:, :n] for p in prods]
            ts = [t + p[:, n:] for t, p in zip(ts, prods)]
        else:
            prods = [_dot(q, t) for q, t in zip(qs, ts)]
            ts = [t + p for t, p in zip(ts, prods)]
    offs = [_dot(a, t) for a, t in zip(a_off, ts)]
    corr = [_dot(t, o) for t, o in zip(ts, offs)]
    return [t - c for t, c in zip(ts, corr)]


def _conv_silu(x_ref, w_ref, blk, rows, n_blk):
    t0 = pl.multiple_of(blk * rows, rows)
    xc = x_ref[0, pl.ds(t0, rows), :].astype(F32)
    prev_start = pl.multiple_of(jnp.maximum(t0 - 8, 0), 8)
    next_start = pl.multiple_of(jnp.minimum(t0 + rows, n_blk * rows - 8), 8)
    prev8 = x_ref[0, pl.ds(prev_start, 8), :].astype(F32)
    next8 = x_ref[0, pl.ds(next_start, 8), :].astype(F32)
    prev8 = jnp.where(blk == 0, 0.0, prev8)
    next8 = jnp.where(blk == n_blk - 1, 0.0, next8)
    r = lax.broadcasted_iota(jnp.int32, xc.shape, 0)
    x_m1 = jnp.where(r == 0, prev8[7:8, :], pltpu.roll(xc, 1, 0))
    x_p1 = jnp.where(r == rows - 1, next8[0:1, :], pltpu.roll(xc, rows - 1, 0))
    x_p2 = jnp.where(r == rows - 2, next8[0:1, :], jnp.where(r == rows - 1, next8[1:2, :], pltpu.roll(xc, rows - 2, 0)))
    w = w_ref[...]
    y = x_m1 * w[0:1, :] + xc * w[1:2, :] + x_p1 * w[2:3, :] + x_p2 * w[3:4, :]
    return _silu(y)


def _gdn_kernel(q_ref, k_ref, v_ref, gcol_ref, grow_ref, z_ref, wq_ref, wk_ref, wv_ref, gain_ref, o_ref,
                u_ref, wqg_ref, kdt_ref, attn_ref, egl_ref, acc_ref):
    s_len = q_ref.shape[1]
    n_chunks = s_len // CHUNK
    head = pl.program_id(1)
    row = lax.broadcasted_iota(jnp.int32, (CHUNK, CHUNK), 0)
    col = lax.broadcasted_iota(jnp.int32, (CHUNK, CHUNK), 1)
    lane = lax.broadcasted_iota(jnp.int32, (CHUNK, LANES), 1)

    def local_group(j):
        ns = [j * LOCAL_UNROLL + g for g in range(LOCAL_UNROLL)]
        t0s = [pl.multiple_of(n * CHUNK, CHUNK) for n in ns]
        qs, ks, vs = [], [], []
        for n in ns:
            q = _conv_silu(q_ref, wq_ref, n, CHUNK, n_chunks)
            k = _conv_silu(k_ref, wk_ref, n, CHUNK, n_chunks)
            vs.append(_conv_silu(v_ref, wv_ref, n, CHUNK, n_chunks))
            qs.append(q * lax.rsqrt(jnp.sum(q * q, axis=-1, keepdims=True) + NORM_EPS) * (HEAD_W ** -0.5))
            ks.append(k * lax.rsqrt(jnp.sum(k * k, axis=-1, keepdims=True) + NORM_EPS))
        qk2s = [_dot_nt(jnp.concatenate([q, k], axis=0), k) for q, k in zip(qs, ks)]
        chains = []
        for g, n in enumerate(ns):
            gates = gcol_ref[0, pl.ds(t0s[g], CHUNK), :]
            for d in range(2):
                sel = head + d * N_HEADS
                gc = jnp.sum(jnp.where(lane == sel, gates, 0.0), axis=1, keepdims=True)
                beta = jnp.sum(jnp.where(lane == sel + 2 * N_HEADS, gates, 0.0), axis=1, keepdims=True)
                gc_row = grow_ref[sel, 0, pl.ds(n, 1), :]
                incl, strict, last = (row >= col, row > col, CHUNK - 1) if d == 0 else (row <= col, row < col, 0)
                decay = jnp.exp(jnp.where(incl, gc - gc_row, MASKED))
                a = jnp.where(strict, qk2s[g][CHUNK:] * beta * decay, 0.0)
                chains.append((g, d, gc, beta, gc_row[:, last:last + 1], decay, a))
        ts = _unit_triangular_inverses([c[-1] for c in chains])
        egcs = [jnp.exp(c[2]) for c in chains]
        uws = [_dot(t, jnp.concatenate([vs[c[0]] * c[3], ks[c[0]] * (c[3] * egc)], axis=1))
               for t, c, egc in zip(ts, chains, egcs)]
        for (g, d, gc, beta, gl, decay, _), egc, uw in zip(chains, egcs, uws):
            t0, n = t0s[g], ns[g]
            u_ref[d, pl.ds(t0, CHUNK), :] = uw[:, :HEAD_W]
            wqg_ref[d, pl.ds(pl.multiple_of(2 * t0, CHUNK), CHUNK), :] = uw[:, HEAD_W:].astype(BF16)
            wqg_ref[d, pl.ds(pl.multiple_of(2 * t0 + CHUNK, CHUNK), CHUNK), :] = (qs[g] * egc).astype(BF16)
            kdt_ref[d, pl.ds(t0, CHUNK), :] = (ks[g] * jnp.exp(gl - gc)).T.astype(BF16)
            attn_ref[d, pl.ds(t0, CHUNK), :] = (qk2s[g][:CHUNK] * decay).astype(BF16)
            egl_ref[d, pl.ds(n, 1), :] = jnp.broadcast_to(jnp.exp(gl), (1, LANES))

    def local_body(j, _):
        local_group(j)
        return 0

    lax.fori_loop(0, n_chunks // LOCAL_UNROLL, local_body, 0)

    def state_steps(nf, nb, states):
        t0s = [pl.multiple_of(n * CHUNK, CHUNK) for n in (nf, nb)]
        rs = [jnp.dot(wqg_ref[d, pl.ds(pl.multiple_of(2 * t0s[d], 2 * CHUNK), 2 * CHUNK), :],
                      states[d].astype(BF16), preferred_element_type=F32) for d in range(2)]
        v_new = [(u_ref[d, pl.ds(t0s[d], CHUNK), :] - rs[d][:CHUNK]).astype(BF16) for d in range(2)]
        upd = [jnp.dot(kdt_ref[d, pl.ds(t0s[d], CHUNK), :], v_new[d], preferred_element_type=F32) for d in range(2)]
        intra = [jnp.dot(attn_ref[d, pl.ds(t0s[d], CHUNK), :], v_new[d], preferred_element_type=F32) for d in range(2)]
        new_states = tuple(states[d] * egl_ref[d, pl.ds(n, 1), :] + upd[d] for d, n in enumerate((nf, nb)))
        outs = [rs[d][CHUNK:] + intra[d] for d in range(2)]
        return outs, new_states

    def finish(t0, o):
        y = o * lax.rsqrt(jnp.mean(o * o, axis=-1, keepdims=True) + NORM_EPS) * gain_ref[...]
        o_ref[0, pl.ds(t0, CHUNK), :] = (y * _silu(z_ref[0, pl.ds(t0, CHUNK), :])).astype(o_ref.dtype)

    def first_half(i, states):
        nf, nb = i, n_chunks - 1 - i
        (o_f, o_b), states = state_steps(nf, nb, states)
        acc_ref[pl.ds(pl.multiple_of(nf * CHUNK, CHUNK), CHUNK), :] = o_f
        acc_ref[pl.ds(pl.multiple_of(nb * CHUNK, CHUNK), CHUNK), :] = o_b
        return states

    def second_half(i, states):
        nf, nb = i, n_chunks - 1 - i
        (o_f, o_b), states = state_steps(nf, nb, states)
        tf = pl.multiple_of(nf * CHUNK, CHUNK)
        tb = pl.multiple_of(nb * CHUNK, CHUNK)
        finish(tf, acc_ref[pl.ds(tf, CHUNK), :] + o_f)
        finish(tb, acc_ref[pl.ds(tb, CHUNK), :] + o_b)
        return states

    zero = jnp.zeros((HEAD_W, HEAD_W), F32)
    states = lax.fori_loop(0, n_chunks // 2, first_half, (zero, zero))
    lax.fori_loop(n_chunks // 2, n_chunks, second_half, states)


def _gdn(qkv, gcol, grow, z, conv_w, gain):
    b, s, _ = qkv.shape
    nh = N_HEADS
    n_chunks = s // CHUNK
    assert s % CHUNK == 0 and n_chunks % 2 == 0 and n_chunks % LOCAL_UNROLL == 0
    seq_spec = lambda off: pl.BlockSpec((1, s, HEAD_W), lambda bi, hi: (bi, 0, hi + off))
    w_spec = lambda off: pl.BlockSpec((CONV_W, HEAD_W), lambda bi, hi: (0, hi + off))
    return pl.pallas_call(
        _gdn_kernel,
        grid=(b, nh),
        in_specs=[
            seq_spec(0), seq_spec(nh), seq_spec(2 * nh),
            pl.BlockSpec((1, s, LANES), lambda bi, hi: (bi, 0, 0)),
            pl.BlockSpec((2 * nh, 1, n_chunks, CHUNK), lambda bi, hi: (0, bi, 0, 0)),
            seq_spec(nh),
            w_spec(0), w_spec(nh), w_spec(2 * nh),
            pl.BlockSpec((1, HEAD_W), lambda bi, hi: (0, 0)),
        ],
        out_specs=seq_spec(0),
        out_shape=jax.ShapeDtypeStruct((b, s, nh * HEAD_W), BF16),
        scratch_shapes=[
            pltpu.VMEM((2, s, HEAD_W), F32),
            pltpu.VMEM((2, 2 * s, HEAD_W), BF16),
            pltpu.VMEM((2, s, CHUNK), BF16),
            pltpu.VMEM((2, s, CHUNK), BF16),
            pltpu.VMEM((2, n_chunks, LANES), F32),
            pltpu.VMEM((s, HEAD_W), F32),
        ],
        compiler_params=_cparams(("parallel", "arbitrary")),
        name="gated_delta",
    )(qkv, qkv, qkv, gcol, grow, z, conv_w, conv_w, conv_w, gain.reshape(1, HEAD_W))


def _merge_kernel(ya_ref, yb_ref, gate_ref, x_ref, wa_ref, wb_ref, wo_ref, g_ref, *out_refs, final, tn):
    d = x_ref.shape[1]
    merged = []
    for j in range(d // tn):
        sl = slice(j * tn, (j + 1) * tn)
        pa = jnp.dot(ya_ref[...], wa_ref[:, sl], preferred_element_type=F32)
        pb = jnp.dot(yb_ref[...], wb_ref[:, sl], preferred_element_type=F32)
        ga = gate_ref[:, j * tn:(j + 1) * tn]
        gb = gate_ref[:, d + j * tn:d + (j + 1) * tn]
        merged.append((ga * pa + gb * pb).astype(BF16))
    merged = jnp.concatenate(merged, axis=1)
    x = x_ref[...] + jnp.dot(merged, wo_ref[...], preferred_element_type=F32)
    y = x * lax.rsqrt(jnp.mean(x * x, axis=-1, keepdims=True) + NORM_EPS) * g_ref[...]
    if final:
        out_refs[0][...] = y
    else:
        out_refs[0][...] = x
        out_refs[1][...] = y.astype(BF16)


def _merge(ya, yb, gates, x, wa, wb, wo, g_next, final, tm=512, tn=256):
    m, d = x.shape
    tm = min(tm, m)
    row = lambda width: pl.BlockSpec((tm, width), lambda i: (i, 0))
    full = lambda r, c: pl.BlockSpec((r, c), lambda i: (0, 0))
    if final:
        out_shape = [jax.ShapeDtypeStruct((m, d), F32)]
        out_specs = [row(d)]
    else:
        out_shape = [jax.ShapeDtypeStruct((m, d), F32), jax.ShapeDtypeStruct((m, d), BF16)]
        out_specs = [row(d), row(d)]
    return pl.pallas_call(
        functools.partial(_merge_kernel, final=final, tn=tn),
        grid=(m // tm,),
        in_specs=[row(d), row(d), row(2 * d), row(d), full(d, d), full(d, d), full(d, d), full(1, d)],
        out_specs=out_specs,
        out_shape=out_shape,
        compiler_params=_cparams(("parallel",)),
        name="merge_out",
    )(ya, yb, gates, x, wa, wb, wo, g_next.reshape(1, d))


def _rope_tables(s):
    inv = ROPE_THETA ** (-jnp.arange(0, HD_QK, 2, dtype=F32) / HD_QK)
    ang = jnp.arange(s, dtype=F32)[:, None] * inv[None, :]
    ang = jnp.concatenate([ang, ang], -1)
    sign = jnp.where(jnp.arange(HD_QK) < HD_QK // 2, -1.0, 1.0).astype(F32)
    cos = jnp.tile(jnp.cos(ang), (1, LANES // HD_QK))
    sin = jnp.tile(jnp.sin(ang) * sign[None, :], (1, LANES // HD_QK))
    return cos, sin


def _trunk(x, norm_g, w_in, conv_w, lam_qk, diff_norm_g, a_log, dt_bias, gdn_norm_g, w_branch, w_out, final_g):
    b, s, d = x.shape
    depth = w_in.shape[0]
    m = b * s
    w_a = N_HEADS * HEAD_W
    qk_w = N_HEADS * 2 * HD_QK
    c_q, c_k, c_v, c_za = 0, qk_w, 2 * qk_w, 2 * qk_w + w_a
    c_qkvb = c_za + w_a
    c_zb = c_qkvb + 3 * w_a
    c_a = c_zb + w_a
    c_b = c_a + 2 * N_HEADS
    c_gate = c_b + 2 * N_HEADS

    cos, sin = _rope_tables(s)
    tm = min(1024, s)
    pos_blocks = s // tm
    rope_specs = [pl.BlockSpec((tm, LANES), lambda i, j: (i % pos_blocks, 0))] * 2

    x2 = x.reshape(m, d)
    h = _rmsnorm(x2, norm_g[0], BF16)
    for l in range(depth):
        w = w_in[l]
        scale = HD_QK ** -0.5
        w_qk = jnp.concatenate([w[:, c_q:c_k] * scale, w[:, c_k:c_v]], axis=1).astype(BF16)
        w_v = w[:, c_v:c_za].astype(BF16)
        w_z = jnp.concatenate([w[:, c_za:c_qkvb], w[:, c_zb:c_a]], axis=1).astype(BF16)
        w_qkvb = w[:, c_qkvb:c_zb].astype(BF16)
        w_ab = jnp.pad(w[:, c_a:c_gate], ((0, 0), (0, LANES - 4 * N_HEADS))).astype(BF16)
        w_gate = w[:, c_gate:].astype(BF16)
        alog_vec = jnp.pad(a_log[l].reshape(1, -1), ((0, 0), (0, LANES - 2 * N_HEADS)))
        dtb_vec = jnp.pad(dt_bias[l].reshape(1, -1), ((0, 0), (0, LANES - 2 * N_HEADS)))

        qk = _proj(_proj_rope_kernel, h, w_qk, (cos, sin), rope_specs, BF16, tm, 512, "proj_qk_rope")
        v_a = _proj(_proj_plain_kernel, h, w_v, (), [], BF16, tm, 512, "proj_v")
        z_ab = _proj(_proj_plain_kernel, h, w_z, (), [], F32, tm, 512, "proj_z")
        qkv_b = _proj(_proj_plain_kernel, h, w_qkvb, (), [], F32, tm, 512, "proj_qkv_delta")
        gcol, grow = _proj_gdn_gates(h, w_ab, alog_vec, dtb_vec, tm)
        merge_gates = _proj(_proj_sigmoid_kernel, h, w_gate, (), [], F32, tm, 512, "proj_merge_gates")

        lam_init = 0.8 - 0.6 * math.exp(-0.3 * l)
        lq = lam_qk[l].astype(F32)
        lam = (jnp.exp(jnp.sum(lq[0] * lq[1])) - jnp.exp(jnp.sum(lq[2] * lq[3])) + lam_init).reshape(1)

        z_ab = z_ab.reshape(b, s, 2 * w_a)
        y_a = _diff_attention(lam, qk.reshape(b, s, 2 * qk_w), v_a.reshape(b, s, w_a), z_ab,
                              diff_norm_g[l], 1.0 - lam_init)
        y_b = _gdn(qkv_b.reshape(b, s, 3 * w_a), gcol.reshape(b, s, LANES),
                   grow.reshape(2 * N_HEADS, b, s // CHUNK, CHUNK), z_ab, conv_w[l], gdn_norm_g[l])

        final = l == depth - 1
        g_next = final_g if final else norm_g[l + 1]
        outs = _merge(y_a.reshape(m, w_a), y_b.reshape(m, w_a), merge_gates, x2,
                      w_branch[l, 0].astype(BF16), w_branch[l, 1].astype(BF16), w_out[l].astype(BF16),
                      g_next, final)
        if final:
            x2 = outs[0]
        else:
            x2, h = outs
    return x2.reshape(b, s, d)


def kernel(x_prompt, x_sample, norm_g, w_in, conv_w, lam_qk, diff_norm_g, a_log, dt_bias, gdn_norm_g, w_branch, w_out, final_g):
    params = (norm_g, w_in, conv_w, lam_qk, diff_norm_g, a_log, dt_bias, gdn_norm_g, w_branch, w_out, final_g)
    return (_trunk(x_prompt, *params), _trunk(x_sample, *params))
```

```python
import functools
import math

import jax
import jax.numpy as jnp
from jax import lax
from jax.experimental import pallas as pl
from jax.experimental.pallas import tpu as pltpu

F32 = jnp.float32
BF16 = jnp.bfloat16

LANES = 128
N_HEADS = 8
HD_QK = 64
HEAD_W = 128
CONV_W = 4
CHUNK = 128
ROPE_THETA = 10000.0
NORM_EPS = 1e-6
SUBLN_EPS = 1e-5
VMEM_LIMIT = 56 * 1024 * 1024
MASKED = -1e30


def _cparams(sem):
    return pltpu.CompilerParams(dimension_semantics=sem, vmem_limit_bytes=VMEM_LIMIT)


def _silu(x):
    return x * (1.0 / (1.0 + jnp.exp(-x)))


def _sigmoid(x):
    return 1.0 / (1.0 + jnp.exp(-x))


def _rmsnorm_kernel(x_ref, g_ref, o_ref, *, eps):
    x = x_ref[...]
    y = x * lax.rsqrt(jnp.mean(x * x, axis=-1, keepdims=True) + eps)
    o_ref[...] = (y * g_ref[...]).astype(o_ref.dtype)


def _rmsnorm(x, g, out_dtype, tm=1024):
    m, d = x.shape
    tm = min(tm, m)
    return pl.pallas_call(
        functools.partial(_rmsnorm_kernel, eps=NORM_EPS),
        grid=(m // tm,),
        in_specs=[pl.BlockSpec((tm, d), lambda i: (i, 0)), pl.BlockSpec((1, d), lambda i: (0, 0))],
        out_specs=pl.BlockSpec((tm, d), lambda i: (i, 0)),
        out_shape=jax.ShapeDtypeStruct((m, d), out_dtype),
        compiler_params=_cparams(("parallel",)),
        name="rmsnorm",
    )(x, g.reshape(1, d))


def _proj_plain_kernel(h_ref, w_ref, o_ref):
    o_ref[...] = jnp.dot(h_ref[...], w_ref[...], preferred_element_type=F32).astype(o_ref.dtype)


def _proj_sigmoid_kernel(h_ref, w_ref, o_ref):
    acc = jnp.dot(h_ref[...], w_ref[...], preferred_element_type=F32)
    o_ref[...] = _sigmoid(acc).astype(o_ref.dtype)


def _proj_rope_kernel(h_ref, w_ref, cos_ref, sin_ref, o_ref):
    acc = jnp.dot(h_ref[...], w_ref[...], preferred_element_type=F32)
    cos = cos_ref[...]
    sin = sin_ref[...]
    lane = lax.broadcasted_iota(jnp.int32, cos.shape, 1)
    first_half = (lane % HD_QK) < (HD_QK // 2)
    for g in range(acc.shape[1] // LANES):
        a = acc[:, g * LANES:(g + 1) * LANES]
        rot = jnp.where(first_half, pltpu.roll(a, LANES - HD_QK // 2, 1), pltpu.roll(a, HD_QK // 2, 1))
        o_ref[:, g * LANES:(g + 1) * LANES] = (a * cos + rot * sin).astype(o_ref.dtype)


def _proj(kernel_fn, h, w, extras, extra_specs, out_dtype, tm, tn, name):
    m, d = h.shape
    n = w.shape[1]
    return pl.pallas_call(
        kernel_fn,
        grid=(m // tm, n // tn),
        in_specs=[pl.BlockSpec((tm, d), lambda i, j: (i, 0)), pl.BlockSpec((d, tn), lambda i, j: (0, j))]
        + extra_specs,
        out_specs=pl.BlockSpec((tm, tn), lambda i, j: (i, j)),
        out_shape=jax.ShapeDtypeStruct((m, n), out_dtype),
        compiler_params=_cparams(("parallel", "arbitrary")),
        name=name,
    )(h, w, *extras)


def _proj_gdn_gate_kernel(h_ref, w_ref, alog_ref, dtb_ref, col_ref, row_ref):
    acc = jnp.dot(h_ref[...], w_ref[...], preferred_element_type=F32)
    lane = lax.broadcasted_iota(jnp.int32, acc.shape, 1)
    x = acc + dtb_ref[...]
    softplus = jnp.maximum(x, 0.0) + jnp.log(1.0 + jnp.exp(-jnp.abs(x)))
    g = jnp.where(lane < 2 * N_HEADS, -jnp.exp(alog_ref[...]) * softplus, 0.0)
    beta = _sigmoid(acc)
    r = lax.broadcasted_iota(jnp.int32, (CHUNK, CHUNK), 0)
    c = lax.broadcasted_iota(jnp.int32, (CHUNK, CHUNK), 1)
    tril = jnp.where(r >= c, 1.0, 0.0).astype(F32)
    lane_c = lax.broadcasted_iota(jnp.int32, (CHUNK, LANES), 1)
    for ci in range(acc.shape[0] // CHUNK):
        rows = slice(ci * CHUNK, (ci + 1) * CHUNK)
        g_c = g[rows]
        prefix = jnp.dot(tril, g_c, preferred_element_type=F32, precision=lax.Precision.HIGHEST)
        suffix = prefix[CHUNK - 1:CHUNK, :] - prefix + g_c
        out = jnp.where(lane_c < N_HEADS, prefix, jnp.where(lane_c < 2 * N_HEADS, suffix, beta[rows]))
        col_ref[rows, :] = out
        row_ref[:, rows] = out.T[:2 * N_HEADS, :]


def _proj_gdn_gates(h, w, alog_vec, dtb_vec, tm):
    m, d = h.shape
    vec = pl.BlockSpec((1, LANES), lambda i: (0, 0))
    return pl.pallas_call(
        _proj_gdn_gate_kernel,
        grid=(m // tm,),
        in_specs=[pl.BlockSpec((tm, d), lambda i: (i, 0)), pl.BlockSpec((d, LANES), lambda i: (0, 0)), vec, vec],
        out_specs=[pl.BlockSpec((tm, LANES), lambda i: (i, 0)), pl.BlockSpec((2 * N_HEADS, tm), lambda i: (0, i))],
        out_shape=[jax.ShapeDtypeStruct((m, LANES), F32), jax.ShapeDtypeStruct((2 * N_HEADS, m), F32)],
        compiler_params=_cparams(("parallel",)),
        name="proj_delta_gates",
    )(h, w, alog_vec, dtb_vec)


def _attn_kernel(lam_ref, q_ref, k_ref, v_ref, z_ref, g_ref, o_ref, s0_ref, s1_ref, acc_ref, m_ref, *, tk, post_scale):
    tq = q_ref.shape[1]
    s_len = k_ref.shape[1]
    nk = s_len // tk
    q = q_ref[0]
    lane = lax.broadcasted_iota(jnp.int32, q.shape, 1)
    zero = jnp.zeros_like(q)
    q2 = jnp.concatenate([jnp.where(lane < HD_QK, q, zero), jnp.where(lane >= HD_QK, q, zero)], axis=0)
    ones = jnp.ones((tk, HEAD_W), BF16)

    def scores(kt, s_ref):
        k = k_ref[0, pl.ds(pl.multiple_of(kt * tk, tk), tk), :]
        s_ref[...] = lax.dot_general(q2, k, (((1,), (1,)), ((), ())), preferred_element_type=F32)

    def update(kt, s_ref):
        s = s_ref[...]
        m_prev = m_ref[...]
        m_new = jnp.maximum(m_prev, jnp.max(s, axis=1, keepdims=True))
        alpha = jnp.exp2(m_prev - m_new)
        p = jnp.exp2(s - m_new).astype(BF16)
        v_aug = jnp.concatenate([v_ref[0, pl.ds(pl.multiple_of(kt * tk, tk), tk), :], ones], axis=1)
        acc_ref[...] = alpha * acc_ref[...] + jnp.dot(p, v_aug, preferred_element_type=F32)
        m_ref[...] = m_new

    m_ref[...] = jnp.full(m_ref.shape, -jnp.inf, F32)
    acc_ref[...] = jnp.zeros(acc_ref.shape, F32)
    scores(0, s0_ref)

    def body(j, _):
        kt = 2 * j
        scores(kt + 1, s1_ref)
        update(kt, s0_ref)
        scores(kt + 2, s0_ref)
        update(kt + 1, s1_ref)
        return 0

    lax.fori_loop(0, nk // 2 - 1, body, 0)
    scores(nk - 1, s1_ref)
    update(nk - 2, s0_ref)
    update(nk - 1, s1_ref)
    acc = acc_ref[...]
    o2 = acc[:, :HEAD_W] / acc[:, HEAD_W:]
    o = o2[:tq] - lam_ref[0] * o2[tq:]
    y = o * lax.rsqrt(jnp.mean(o * o, axis=-1, keepdims=True) + SUBLN_EPS) * g_ref[...] * post_scale
    o_ref[0] = (y * _silu(z_ref[0].astype(F32))).astype(o_ref.dtype)


def _diff_attention(lam, qk, v, z, gain, post_scale, tq=512, tk=1024):
    b, s, w = v.shape
    nh = w // HEAD_W
    tq = min(tq, s)
    tk = min(tk, s // 2)
    assert s % (2 * tk) == 0 and s % tq == 0
    return pl.pallas_call(
        functools.partial(_attn_kernel, tk=tk, post_scale=post_scale),
        scratch_shapes=[
            pltpu.VMEM((2 * tq, tk), F32),
            pltpu.VMEM((2 * tq, tk), F32),
            pltpu.VMEM((2 * tq, 2 * HEAD_W), F32),
            pltpu.VMEM((2 * tq, 1), F32),
        ],
        grid=(b, nh, s // tq),
        in_specs=[
            pl.BlockSpec(memory_space=pltpu.SMEM),
            pl.BlockSpec((1, tq, HEAD_W), lambda bi, hi, qi: (bi, qi, hi)),
            pl.BlockSpec((1, s, HEAD_W), lambda bi, hi, qi: (bi, 0, nh + hi)),
            pl.BlockSpec((1, s, HEAD_W), lambda bi, hi, qi: (bi, 0, hi)),
            pl.BlockSpec((1, tq, HEAD_W), lambda bi, hi, qi: (bi, qi, hi)),
            pl.BlockSpec((1, HEAD_W), lambda bi, hi, qi: (0, 0)),
        ],
        out_specs=pl.BlockSpec((1, tq, HEAD_W), lambda bi, hi, qi: (bi, qi, hi)),
        out_shape=jax.ShapeDtypeStruct((b, s, w), BF16),
        compiler_params=_cparams(("parallel", "parallel", "arbitrary")),
        name="diff_attention",
    )(lam, qk, qk, v, z, gain.reshape(1, HEAD_W))


def _dot(a, b):
    return jnp.dot(a.astype(BF16), b.astype(BF16), preferred_element_type=F32)


def _dot_nt(a, b):
    return lax.dot_general(a.astype(BF16), b.astype(BF16), (((1,), (1,)), ((), ())), preferred_element_type=F32)


INV_BLOCK = 64
LOCAL_UNROLL = 4


def _unit_triangular_inverses(mats):
    n = mats[0].shape[0]
    row = lax.broadcasted_iota(jnp.int32, (n, n), 0)
    col = lax.broadcasted_iota(jnp.int32, (n, n), 1)
    same_block = (row // INV_BLOCK) == (col // INV_BLOCK)
    eye = jnp.where(row == col, 1.0, 0.0).astype(F32)
    a_diag = [jnp.where(same_block, a, 0.0) for a in mats]
    a_off = [jnp.where(same_block, 0.0, a) for a in mats]
    ts = [eye - a for a in a_diag]
    qs = [_dot(a, a) for a in a_diag]
    levels = int(math.log2(INV_BLOCK)) - 1
    for level in range(levels):
        if level < levels - 1:
            prods = [_dot(q, jnp.concatenate([q, t], axis=1)) for q, t in zip(qs, ts)]
            qs = [p[:, :n] for p in prods]
            ts = [t + p[:, n:] for t, p in zip(ts, prods)]
        else:
            prods = [_dot(q, t) for q, t in zip(qs, ts)]
            ts = [t + p for t, p in zip(ts, prods)]
    offs = [_dot(a, t) for a, t in zip(a_off, ts)]
    corr = [_dot(t, o) for t, o in zip(ts, offs)]
    return [t - c for t, c in zip(ts, corr)]


def _conv_silu(x_ref, w_ref, blk, rows, n_blk):
    halo = 16
    t0 = pl.multiple_of(blk * rows, rows)
    xc = x_ref[0, pl.ds(t0, rows), :].astype(F32)
    prev_start = pl.multiple_of(jnp.maximum(t0 - halo, 0), halo)
    next_start = pl.multiple_of(jnp.minimum(t0 + rows, n_blk * rows - halo), halo)
    prev = jnp.where(blk == 0, 0.0, x_ref[0, pl.ds(prev_start, halo), :].astype(F32))
    nxt = jnp.where(blk == n_blk - 1, 0.0, x_ref[0, pl.ds(next_start, halo), :].astype(F32))
    xx = jnp.concatenate([prev, xc, nxt], axis=0)
    n = rows + 2 * halo
    x_m1 = pltpu.roll(xx, 1, 0)[halo:halo + rows]
    x_p1 = pltpu.roll(xx, n - 1, 0)[halo:halo + rows]
    x_p2 = pltpu.roll(xx, n - 2, 0)[halo:halo + rows]
    w = w_ref[...]
    y = x_m1 * w[0:1, :] + xc * w[1:2, :] + x_p1 * w[2:3, :] + x_p2 * w[3:4, :]
    return _silu(y)


def _gdn_kernel(q_ref, k_ref, v_ref, gcol_ref, grow_ref, z_ref, wq_ref, wk_ref, wv_ref, gain_ref, o_ref,
                u_ref, wqg_ref, kdt_ref, attn_ref, egl_ref, acc_ref):
    s_len = q_ref.shape[1]
    n_chunks = s_len // CHUNK
    head = pl.program_id(1)
    row = lax.broadcasted_iota(jnp.int32, (CHUNK, CHUNK), 0)
    col = lax.broadcasted_iota(jnp.int32, (CHUNK, CHUNK), 1)
    lane = lax.broadcasted_iota(jnp.int32, (CHUNK, LANES), 1)

    def local_group(j):
        ns = [j * LOCAL_UNROLL + g for g in range(LOCAL_UNROLL)]
        t0s = [pl.multiple_of(n * CHUNK, CHUNK) for n in ns]
        qs, ks, vs = [], [], []
        for n in ns:
            q = _conv_silu(q_ref, wq_ref, n, CHUNK, n_chunks)
            k = _conv_silu(k_ref, wk_ref, n, CHUNK, n_chunks)
            vs.append(_conv_silu(v_ref, wv_ref, n, CHUNK, n_chunks))
            qs.append(q * lax.rsqrt(jnp.sum(q * q, axis=-1, keepdims=True) + NORM_EPS) * (HEAD_W ** -0.5))
            ks.append(k * lax.rsqrt(jnp.sum(k * k, axis=-1, keepdims=True) + NORM_EPS))
        qk2s = [_dot_nt(jnp.concatenate([q, k], axis=0), k) for q, k in zip(qs, ks)]
        chains = []
        for g, n in enumerate(ns):
            gates = gcol_ref[0, pl.ds(t0s[g], CHUNK), :]
            for d in range(2):
                sel = head + d * N_HEADS
                gc = jnp.sum(jnp.where(lane == sel, gates, 0.0), axis=1, keepdims=True)
                beta = jnp.sum(jnp.where(lane == sel + 2 * N_HEADS, gates, 0.0), axis=1, keepdims=True)
                gc_row = grow_ref[sel, 0, pl.ds(n, 1), :]
                incl, strict, last = (row >= col, row > col, CHUNK - 1) if d == 0 else (row <= col, row < col, 0)
                decay = jnp.exp(jnp.where(incl, gc - gc_row, MASKED))
                a = jnp.where(strict, qk2s[g][CHUNK:] * beta * decay, 0.0)
                chains.append((g, d, gc, beta, gc_row[:, last:last + 1], decay, a))
        ts = _unit_triangular_inverses([c[-1] for c in chains])
        egcs = [jnp.exp(c[2]) for c in chains]
        uws = [_dot(t, jnp.concatenate([vs[c[0]] * c[3], ks[c[0]] * (c[3] * egc)], axis=1))
               for t, c, egc in zip(ts, chains, egcs)]
        for (g, d, gc, beta, gl, decay, _), egc, uw in zip(chains, egcs, uws):
            t0, n = t0s[g], ns[g]
            u_ref[d, pl.ds(t0, CHUNK), :] = uw[:, :HEAD_W]
            wqg_ref[d, pl.ds(pl.multiple_of(2 * t0, CHUNK), CHUNK), :] = uw[:, HEAD_W:].astype(BF16)
            wqg_ref[d, pl.ds(pl.multiple_of(2 * t0 + CHUNK, CHUNK), CHUNK), :] = (qs[g] * egc).astype(BF16)
            kdt_ref[d, pl.ds(t0, CHUNK), :] = (ks[g] * jnp.exp(gl - gc)).T.astype(BF16)
            attn_ref[d, pl.ds(t0, CHUNK), :] = (qk2s[g][:CHUNK] * decay).astype(BF16)
            egl_ref[d, pl.ds(n, 1), :] = jnp.broadcast_to(jnp.exp(gl), (1, LANES))

    def local_body(j, _):
        local_group(j)
        return 0

    lax.fori_loop(0, n_chunks // LOCAL_UNROLL, local_body, 0)

    def state_steps(nf, nb, states):
        t0s = [pl.multiple_of(n * CHUNK, CHUNK) for n in (nf, nb)]
        rs = [jnp.dot(wqg_ref[d, pl.ds(pl.multiple_of(2 * t0s[d], 2 * CHUNK), 2 * CHUNK), :],
                      states[d].astype(BF16), preferred_element_type=F32) for d in range(2)]
        v_new = [(u_ref[d, pl.ds(t0s[d], CHUNK), :] - rs[d][:CHUNK]).astype(BF16) for d in range(2)]
        upd = [jnp.dot(kdt_ref[d, pl.ds(t0s[d], CHUNK), :], v_new[d], preferred_element_type=F32) for d in range(2)]
        intra = [jnp.dot(attn_ref[d, pl.ds(t0s[d], CHUNK), :], v_new[d], preferred_element_type=F32) for d in range(2)]
        new_states = tuple(states[d] * egl_ref[d, pl.ds(n, 1), :] + upd[d] for d, n in enumerate((nf, nb)))
        outs = [rs[d][CHUNK:] + intra[d] for d in range(2)]
        return outs, new_states

    def finish(t0, o):
        y = o * lax.rsqrt(jnp.mean(o * o, axis=-1, keepdims=True) + NORM_EPS) * gain_ref[...]
        o_ref[0, pl.ds(t0, CHUNK), :] = (y * _silu(z_ref[0, pl.ds(t0, CHUNK), :].astype(F32))).astype(o_ref.dtype)

    def first_half(i, states):
        nf, nb = i, n_chunks - 1 - i
        (o_f, o_b), states = state_steps(nf, nb, states)
        acc_ref[pl.ds(pl.multiple_of(nf * CHUNK, CHUNK), CHUNK), :] = o_f
        acc_ref[pl.ds(pl.multiple_of(nb * CHUNK, CHUNK), CHUNK), :] = o_b
        return states

    def second_half(i, states):
        nf, nb = i, n_chunks - 1 - i
        (o_f, o_b), states = state_steps(nf, nb, states)
        tf = pl.multiple_of(nf * CHUNK, CHUNK)
        tb = pl.multiple_of(nb * CHUNK, CHUNK)
        finish(tf, acc_ref[pl.ds(tf, CHUNK), :] + o_f)
        finish(tb, acc_ref[pl.ds(tb, CHUNK), :] + o_b)
        return states

    zero = jnp.zeros((HEAD_W, HEAD_W), F32)
    states = lax.fori_loop(0, n_chunks // 2, first_half, (zero, zero))
    lax.fori_loop(n_chunks // 2, n_chunks, second_half, states)


def _gdn(qkv, gcol, grow, z, conv_w, gain):
    b, s, _ = qkv.shape
    nh = N_HEADS
    n_chunks = s // CHUNK
    assert s % CHUNK == 0 and n_chunks % 2 == 0 and n_chunks % LOCAL_UNROLL == 0
    seq_spec = lambda off: pl.BlockSpec((1, s, HEAD_W), lambda bi, hi: (bi, 0, hi + off))
    w_spec = lambda off: pl.BlockSpec((CONV_W, HEAD_W), lambda bi, hi: (0, hi + off))
    return pl.pallas_call(
        _gdn_kernel,
        grid=(b, nh),
        in_specs=[
            seq_spec(0), seq_spec(nh), seq_spec(2 * nh),
            pl.BlockSpec((1, s, LANES), lambda bi, hi: (bi, 0, 0)),
            pl.BlockSpec((2 * nh, 1, n_chunks, CHUNK), lambda bi, hi: (0, bi, 0, 0)),
            seq_spec(nh),
            w_spec(0), w_spec(nh), w_spec(2 * nh),
            pl.BlockSpec((1, HEAD_W), lambda bi, hi: (0, 0)),
        ],
        out_specs=seq_spec(0),
        out_shape=jax.ShapeDtypeStruct((b, s, nh * HEAD_W), BF16),
        scratch_shapes=[
            pltpu.VMEM((2, s, HEAD_W), F32),
            pltpu.VMEM((2, 2 * s, HEAD_W), BF16),
            pltpu.VMEM((2, s, CHUNK), BF16),
            pltpu.VMEM((2, s, CHUNK), BF16),
            pltpu.VMEM((2, n_chunks, LANES), F32),
            pltpu.VMEM((s, HEAD_W), F32),
        ],
        compiler_params=_cparams(("parallel", "arbitrary")),
        name="gated_delta",
    )(qkv, qkv, qkv, gcol, grow, z, conv_w, conv_w, conv_w, gain.reshape(1, HEAD_W))


def _merge_kernel(ya_ref, yb_ref, gate_ref, x_ref, wa_ref, wb_ref, wo_ref, g_ref, *out_refs, final, tn):
    d = x_ref.shape[1]
    merged = []
    for j in range(d // tn):
        sl = slice(j * tn, (j + 1) * tn)
        pa = jnp.dot(ya_ref[...], wa_ref[:, sl], preferred_element_type=F32)
        pb = jnp.dot(yb_ref[...], wb_ref[:, sl], preferred_element_type=F32)
        ga = gate_ref[:, j * tn:(j + 1) * tn].astype(F32)
        gb = gate_ref[:, d + j * tn:d + (j + 1) * tn].astype(F32)
        merged.append((ga * pa + gb * pb).astype(BF16))
    merged = jnp.concatenate(merged, axis=1)
    x = x_ref[...] + jnp.dot(merged, wo_ref[...], preferred_element_type=F32)
    y = x * lax.rsqrt(jnp.mean(x * x, axis=-1, keepdims=True) + NORM_EPS) * g_ref[...]
    if final:
        out_refs[0][...] = y
    else:
        out_refs[0][...] = x
        out_refs[1][...] = y.astype(BF16)


def _merge(ya, yb, gates, x, wa, wb, wo, g_next, final, tm=512, tn=256):
    m, d = x.shape
    tm = min(tm, m)
    row = lambda width: pl.BlockSpec((tm, width), lambda i: (i, 0))
    full = lambda r, c: pl.BlockSpec((r, c), lambda i: (0, 0))
    if final:
        out_shape = [jax.ShapeDtypeStruct((m, d), F32)]
        out_specs = [row(d)]
    else:
        out_shape = [jax.ShapeDtypeStruct((m, d), F32), jax.ShapeDtypeStruct((m, d), BF16)]
        out_specs = [row(d), row(d)]
    return pl.pallas_call(
        functools.partial(_merge_kernel, final=final, tn=tn),
        grid=(m // tm,),
        in_specs=[row(d), row(d), row(2 * d), row(d), full(d, d), full(d, d), full(d, d), full(1, d)],
        out_specs=out_specs,
        out_shape=out_shape,
        compiler_params=_cparams(("parallel",)),
        name="merge_out",
    )(ya, yb, gates, x, wa, wb, wo, g_next.reshape(1, d))


def _rope_tables(s):
    inv = ROPE_THETA ** (-jnp.arange(0, HD_QK, 2, dtype=F32) / HD_QK)
    ang = jnp.arange(s, dtype=F32)[:, None] * inv[None, :]
    ang = jnp.concatenate([ang, ang], -1)
    sign = jnp.where(jnp.arange(HD_QK) < HD_QK // 2, -1.0, 1.0).astype(F32)
    cos = jnp.tile(jnp.cos(ang), (1, LANES // HD_QK))
    sin = jnp.tile(jnp.sin(ang) * sign[None, :], (1, LANES // HD_QK))
    return cos, sin


def _trunk(x, norm_g, w_in, conv_w, lam_qk, diff_norm_g, a_log, dt_bias, gdn_norm_g, w_branch, w_out, final_g):
    b, s, d = x.shape
    depth = w_in.shape[0]
    m = b * s
    w_a = N_HEADS * HEAD_W
    qk_w = N_HEADS * 2 * HD_QK
    c_q, c_k, c_v, c_za = 0, qk_w, 2 * qk_w, 2 * qk_w + w_a
    c_qkvb = c_za + w_a
    c_zb = c_qkvb + 3 * w_a
    c_a = c_zb + w_a
    c_b = c_a + 2 * N_HEADS
    c_gate = c_b + 2 * N_HEADS

    cos, sin = _rope_tables(s)
    tm = min(1024, s)
    pos_blocks = s // tm
    rope_specs = [pl.BlockSpec((tm, LANES), lambda i, j: (i % pos_blocks, 0))] * 2

    x2 = x.reshape(m, d)
    h = _rmsnorm(x2, norm_g[0], BF16)
    for l in range(depth):
        w = w_in[l]
        scale = HD_QK ** -0.5 * math.log2(math.e)
        w_qk = jnp.concatenate([w[:, c_q:c_k] * scale, w[:, c_k:c_v]], axis=1).astype(BF16)
        w_v = w[:, c_v:c_za].astype(BF16)
        w_z = jnp.concatenate([w[:, c_za:c_qkvb], w[:, c_zb:c_a]], axis=1).astype(BF16)
        w_qkvb = w[:, c_qkvb:c_zb].astype(BF16)
        w_ab = jnp.pad(w[:, c_a:c_gate], ((0, 0), (0, LANES - 4 * N_HEADS))).astype(BF16)
        w_gate = w[:, c_gate:].astype(BF16)
        alog_vec = jnp.pad(a_log[l].reshape(1, -1), ((0, 0), (0, LANES - 2 * N_HEADS)))
        dtb_vec = jnp.pad(dt_bias[l].reshape(1, -1), ((0, 0), (0, LANES - 2 * N_HEADS)))

        qk = _proj(_proj_rope_kernel, h, w_qk, (cos, sin), rope_specs, BF16, tm, 512, "proj_qk_rope")
        v_a = _proj(_proj_plain_kernel, h, w_v, (), [], BF16, tm, 512, "proj_v")
        z_ab = _proj(_proj_plain_kernel, h, w_z, (), [], BF16, tm, 512, "proj_z")
        qkv_b = _proj(_proj_plain_kernel, h, w_qkvb, (), [], BF16, tm, 512, "proj_qkv_delta")
        gcol, grow = _proj_gdn_gates(h, w_ab, alog_vec, dtb_vec, tm)
        merge_gates = _proj(_proj_sigmoid_kernel, h, w_gate, (), [], BF16, tm, 512, "proj_merge_gates")

        lam_init = 0.8 - 0.6 * math.exp(-0.3 * l)
        lq = lam_qk[l].astype(F32)
        lam = (jnp.exp(jnp.sum(lq[0] * lq[1])) - jnp.exp(jnp.sum(lq[2] * lq[3])) + lam_init).reshape(1)

        z_ab = z_ab.reshape(b, s, 2 * w_a)
        y_a = _diff_attention(lam, qk.reshape(b, s, 2 * qk_w), v_a.reshape(b, s, w_a), z_ab,
                              diff_norm_g[l], 1.0 - lam_init)
        y_b = _gdn(qkv_b.reshape(b, s, 3 * w_a), gcol.reshape(b, s, LANES),
                   grow.reshape(2 * N_HEADS, b, s // CHUNK, CHUNK), z_ab, conv_w[l], gdn_norm_g[l])

        final = l == depth - 1
        g_next = final_g if final else norm_g[l + 1]
        outs = _merge(y_a.reshape(m, w_a), y_b.reshape(m, w_a), merge_gates, x2,
                      w_branch[l, 0].astype(BF16), w_branch[l, 1].astype(BF16), w_out[l].astype(BF16),
                      g_next, final)
        if final:
            x2 = outs[0]
        else:
            x2, h = outs
    return x2.reshape(b, s, d)


def kernel(x_prompt, x_sample, norm_g, w_in, conv_w, lam_qk, diff_norm_g, a_log, dt_bias, gdn_norm_g, w_branch, w_out, final_g):
    params = (norm_g, w_in, conv_w, lam_qk, diff_norm_g, a_log, dt_bias, gdn_norm_g, w_branch, w_out, final_g)
    return (_trunk(x_prompt, *params), _trunk(x_sample, *params))
```

```python
import functools
import math

import jax
import jax.numpy as jnp
from jax import lax
from jax.experimental import pallas as pl
from jax.experimental.pallas import tpu as pltpu

F32 = jnp.float32
BF16 = jnp.bfloat16

LANES = 128
N_HEADS = 8
HD_QK = 64
HEAD_W = 128
CONV_W = 4
CHUNK = 128
ROPE_THETA = 10000.0
NORM_EPS = 1e-6
SUBLN_EPS = 1e-5
VMEM_LIMIT = 56 * 1024 * 1024
MASKED = -1e30


def _cparams(sem):
    return pltpu.CompilerParams(dimension_semantics=sem, vmem_limit_bytes=VMEM_LIMIT)


def _silu(x):
    return x * (1.0 / (1.0 + jnp.exp(-x)))


def _sigmoid(x):
    return 1.0 / (1.0 + jnp.exp(-x))


def _rmsnorm_kernel(x_ref, g_ref, o_ref, *, eps):
    x = x_ref[...]
    y = x * lax.rsqrt(jnp.mean(x * x, axis=-1, keepdims=True) + eps)
    o_ref[...] = (y * g_ref[...]).astype(o_ref.dtype)


def _rmsnorm(x, g, out_dtype, tm=1024):
    m, d = x.shape
    tm = min(tm, m)
    return pl.pallas_call(
        functools.partial(_rmsnorm_kernel, eps=NORM_EPS),
        grid=(m // tm,),
        in_specs=[pl.BlockSpec((tm, d), lambda i: (i, 0)), pl.BlockSpec((1, d), lambda i: (0, 0))],
        out_specs=pl.BlockSpec((tm, d), lambda i: (i, 0)),
        out_shape=jax.ShapeDtypeStruct((m, d), out_dtype),
        compiler_params=_cparams(("parallel",)),
        name="rmsnorm",
    )(x, g.reshape(1, d))


PROJ_COLS = 512


def _proj_plain_kernel(h_ref, w_ref, o_ref):
    h = h_ref[...]
    for j in range(o_ref.shape[1] // PROJ_COLS):
        cols = slice(j * PROJ_COLS, (j + 1) * PROJ_COLS)
        o_ref[:, cols] = jnp.dot(h, w_ref[:, cols], preferred_element_type=F32).astype(o_ref.dtype)


def _proj_sigmoid_kernel(h_ref, w_ref, o_ref):
    h = h_ref[...]
    for j in range(o_ref.shape[1] // PROJ_COLS):
        cols = slice(j * PROJ_COLS, (j + 1) * PROJ_COLS)
        acc = jnp.dot(h, w_ref[:, cols], preferred_element_type=F32)
        o_ref[:, cols] = _sigmoid(acc).astype(o_ref.dtype)


def _proj_rope_kernel(h_ref, w_ref, cos_ref, sin_ref, o_ref):
    h = h_ref[...]
    cos = cos_ref[...]
    sin = sin_ref[...]
    lane = lax.broadcasted_iota(jnp.int32, cos.shape, 1)
    first_half = (lane % HD_QK) < (HD_QK // 2)
    for j in range(o_ref.shape[1] // PROJ_COLS):
        acc = jnp.dot(h, w_ref[:, j * PROJ_COLS:(j + 1) * PROJ_COLS], preferred_element_type=F32)
        for g in range(PROJ_COLS // LANES):
            a = acc[:, g * LANES:(g + 1) * LANES]
            rot = jnp.where(first_half, pltpu.roll(a, LANES - HD_QK // 2, 1), pltpu.roll(a, HD_QK // 2, 1))
            lanes = slice(j * PROJ_COLS + g * LANES, j * PROJ_COLS + (g + 1) * LANES)
            o_ref[:, lanes] = (a * cos + rot * sin).astype(o_ref.dtype)


def _proj(kernel_fn, h, w, extras, extra_specs, out_dtype, tm, name):
    m, d = h.shape
    n = w.shape[1]
    return pl.pallas_call(
        kernel_fn,
        grid=(m // tm,),
        in_specs=[pl.BlockSpec((tm, d), lambda i: (i, 0)), pl.BlockSpec((d, n), lambda i: (0, 0))] + extra_specs,
        out_specs=pl.BlockSpec((tm, n), lambda i: (i, 0)),
        out_shape=jax.ShapeDtypeStruct((m, n), out_dtype),
        compiler_params=_cparams(("parallel",)),
        name=name,
    )(h, w, *extras)


def _proj_gdn_gate_kernel(h_ref, w_ref, alog_ref, dtb_ref, col_ref, row_ref):
    acc = jnp.dot(h_ref[...], w_ref[...], preferred_element_type=F32)
    lane = lax.broadcasted_iota(jnp.int32, acc.shape, 1)
    x = acc + dtb_ref[...]
    softplus = jnp.maximum(x, 0.0) + jnp.log(1.0 + jnp.exp(-jnp.abs(x)))
    g = jnp.where(lane < 2 * N_HEADS, -jnp.exp(alog_ref[...]) * softplus, 0.0)
    beta = _sigmoid(acc)
    r = lax.broadcasted_iota(jnp.int32, (CHUNK, CHUNK), 0)
    c = lax.broadcasted_iota(jnp.int32, (CHUNK, CHUNK), 1)
    tril = jnp.where(r >= c, 1.0, 0.0).astype(F32)
    lane_c = lax.broadcasted_iota(jnp.int32, (CHUNK, LANES), 1)
    for ci in range(acc.shape[0] // CHUNK):
        rows = slice(ci * CHUNK, (ci + 1) * CHUNK)
        g_c = g[rows]
        prefix = jnp.dot(tril, g_c, preferred_element_type=F32, precision=lax.Precision.HIGHEST)
        suffix = prefix[CHUNK - 1:CHUNK, :] - prefix + g_c
        out = jnp.where(lane_c < N_HEADS, prefix, jnp.where(lane_c < 2 * N_HEADS, suffix, beta[rows]))
        col_ref[rows, :] = out
        row_ref[:, rows] = out.T[:2 * N_HEADS, :]


def _proj_gdn_gates(h, w, alog_vec, dtb_vec, tm):
    m, d = h.shape
    vec = pl.BlockSpec((1, LANES), lambda i: (0, 0))
    return pl.pallas_call(
        _proj_gdn_gate_kernel,
        grid=(m // tm,),
        in_specs=[pl.BlockSpec((tm, d), lambda i: (i, 0)), pl.BlockSpec((d, LANES), lambda i: (0, 0)), vec, vec],
        out_specs=[pl.BlockSpec((tm, LANES), lambda i: (i, 0)), pl.BlockSpec((2 * N_HEADS, tm), lambda i: (0, i))],
        out_shape=[jax.ShapeDtypeStruct((m, LANES), F32), jax.ShapeDtypeStruct((2 * N_HEADS, m), F32)],
        compiler_params=_cparams(("parallel",)),
        name="proj_delta_gates",
    )(h, w, alog_vec, dtb_vec)


def _attn_kernel(lam_ref, q_ref, k_ref, v_ref, z_ref, g_ref, o_ref, s0_ref, s1_ref, acc_ref, m_ref, *, tk, post_scale):
    tq = q_ref.shape[1]
    s_len = k_ref.shape[1]
    nk = s_len // tk
    q = q_ref[0]
    lane = lax.broadcasted_iota(jnp.int32, q.shape, 1)
    zero = jnp.zeros_like(q)
    q2 = jnp.concatenate([jnp.where(lane < HD_QK, q, zero), jnp.where(lane >= HD_QK, q, zero)], axis=0)
    ones = jnp.ones((tk, HEAD_W), BF16)

    def scores(kt, s_ref):
        k = k_ref[0, pl.ds(pl.multiple_of(kt * tk, tk), tk), :]
        s_ref[...] = lax.dot_general(q2, k, (((1,), (1,)), ((), ())), preferred_element_type=F32)

    def update(kt, s_ref):
        s = s_ref[...]
        m_prev = m_ref[...]
        m_new = jnp.maximum(m_prev, jnp.max(s, axis=1, keepdims=True))
        alpha = jnp.exp2(m_prev - m_new)
        p = jnp.exp2(s - m_new).astype(BF16)
        v_aug = jnp.concatenate([v_ref[0, pl.ds(pl.multiple_of(kt * tk, tk), tk), :], ones], axis=1)
        acc_ref[...] = alpha * acc_ref[...] + jnp.dot(p, v_aug, preferred_element_type=F32)
        m_ref[...] = m_new

    m_ref[...] = jnp.full(m_ref.shape, -jnp.inf, F32)
    acc_ref[...] = jnp.zeros(acc_ref.shape, F32)
    scores(0, s0_ref)

    def body(j, _):
        kt = 2 * j
        scores(kt + 1, s1_ref)
        update(kt, s0_ref)
        scores(kt + 2, s0_ref)
        update(kt + 1, s1_ref)
        return 0

    lax.fori_loop(0, nk // 2 - 1, body, 0)
    scores(nk - 1, s1_ref)
    update(nk - 2, s0_ref)
    update(nk - 1, s1_ref)
    acc = acc_ref[...]
    o2 = acc[:, :HEAD_W] / acc[:, HEAD_W:]
    o = o2[:tq] - lam_ref[0] * o2[tq:]
    y = o * lax.rsqrt(jnp.mean(o * o, axis=-1, keepdims=True) + SUBLN_EPS) * g_ref[...] * post_scale
    o_ref[0] = (y * _silu(z_ref[0].astype(F32))).astype(o_ref.dtype)


def _diff_attention(lam, qk, v, z, gain, post_scale, tq=512, tk=1024):
    b, s, w = v.shape
    nh = w // HEAD_W
    tq = min(tq, s)
    tk = min(tk, s // 2)
    assert s % (2 * tk) == 0 and s % tq == 0
    return pl.pallas_call(
        functools.partial(_attn_kernel, tk=tk, post_scale=post_scale),
        scratch_shapes=[
            pltpu.VMEM((2 * tq, tk), F32),
            pltpu.VMEM((2 * tq, tk), F32),
            pltpu.VMEM((2 * tq, 2 * HEAD_W), F32),
            pltpu.VMEM((2 * tq, 1), F32),
        ],
        grid=(b, nh, s // tq),
        in_specs=[
            pl.BlockSpec(memory_space=pltpu.SMEM),
            pl.BlockSpec((1, tq, HEAD_W), lambda bi, hi, qi: (bi, qi, hi)),
            pl.BlockSpec((1, s, HEAD_W), lambda bi, hi, qi: (bi, 0, nh + hi)),
            pl.BlockSpec((1, s, HEAD_W), lambda bi, hi, qi: (bi, 0, hi)),
            pl.BlockSpec((1, tq, HEAD_W), lambda bi, hi, qi: (bi, qi, hi)),
            pl.BlockSpec((1, HEAD_W), lambda bi, hi, qi: (0, 0)),
        ],
        out_specs=pl.BlockSpec((1, tq, HEAD_W), lambda bi, hi, qi: (bi, qi, hi)),
        out_shape=jax.ShapeDtypeStruct((b, s, w), BF16),
        compiler_params=_cparams(("parallel", "parallel", "arbitrary")),
        name="diff_attention",
    )(lam, qk, qk, v, z, gain.reshape(1, HEAD_W))


def _dot(a, b):
    return jnp.dot(a.astype(BF16), b.astype(BF16), preferred_element_type=F32)


def _dot_nt(a, b):
    return lax.dot_general(a.astype(BF16), b.astype(BF16), (((1,), (1,)), ((), ())), preferred_element_type=F32)


INV_BLOCK = 64
LOCAL_UNROLL = 4


def _unit_triangular_inverses(mats):
    n = mats[0].shape[0]
    row = lax.broadcasted_iota(jnp.int32, (n, n), 0)
    col = lax.broadcasted_iota(jnp.int32, (n, n), 1)
    same_block = (row // INV_BLOCK) == (col // INV_BLOCK)
    eye = jnp.where(row == col, 1.0, 0.0).astype(F32)
    a_diag = [jnp.where(same_block, a, 0.0) for a in mats]
    a_off = [jnp.where(same_block, 0.0, a) for a in mats]
    ts = [eye - a for a in a_diag]
    qs = [_dot(a, a) for a in a_diag]
    levels = int(math.log2(INV_BLOCK)) - 1
    for level in range(levels):
        if level < levels - 1:
            prods = [_dot(q, jnp.concatenate([q, t], axis=1)) for q, t in zip(qs, ts)]
            qs = [p[:, :n] for p in prods]
            ts = [t + p[:, n:] for t, p in zip(ts, prods)]
        else:
            prods = [_dot(q, t) for q, t in zip(qs, ts)]
            ts = [t + p for t, p in zip(ts, prods)]
    offs = [_dot(a, t) for a, t in zip(a_off, ts)]
    corr = [_dot(t, o) for t, o in zip(ts, offs)]
    return [t - c for t, c in zip(ts, corr)]


def _conv_silu(x_ref, w_ref, blk, rows, n_blk):
    halo = 16
    t0 = pl.multiple_of(blk * rows, rows)
    xc = x_ref[0, pl.ds(t0, rows), :].astype(F32)
    prev_start = pl.multiple_of(jnp.maximum(t0 - halo, 0), halo)
    next_start = pl.multiple_of(jnp.minimum(t0 + rows, n_blk * rows - halo), halo)
    prev = jnp.where(blk == 0, 0.0, x_ref[0, pl.ds(prev_start, halo), :].astype(F32))
    nxt = jnp.where(blk == n_blk - 1, 0.0, x_ref[0, pl.ds(next_start, halo), :].astype(F32))
    xx = jnp.concatenate([prev, xc, nxt], axis=0)
    n = rows + 2 * halo
    x_m1 = pltpu.roll(xx, 1, 0)[halo:halo + rows]
    x_p1 = pltpu.roll(xx, n - 1, 0)[halo:halo + rows]
    x_p2 = pltpu.roll(xx, n - 2, 0)[halo:halo + rows]
    w = w_ref[...]
    y = x_m1 * w[0:1, :] + xc * w[1:2, :] + x_p1 * w[2:3, :] + x_p2 * w[3:4, :]
    return _silu(y)


def _gdn_kernel(q_ref, k_ref, v_ref, gcol_ref, grow_ref, z_ref, wq_ref, wk_ref, wv_ref, gain_ref, o_ref,
                qkv0_ref, qkv1_ref, u_ref, wqg_ref, sm_ref, sn_ref, attn_ref, egl_ref, dir_ref):
    s_len = q_ref.shape[1]
    n_chunks = s_len // CHUNK
    head = pl.program_id(1)
    row = lax.broadcasted_iota(jnp.int32, (CHUNK, CHUNK), 0)
    col = lax.broadcasted_iota(jnp.int32, (CHUNK, CHUNK), 1)
    lane = lax.broadcasted_iota(jnp.int32, (CHUNK, LANES), 1)

    def conv_group(j, qkv_ref):
        for g in range(LOCAL_UNROLL):
            n = j * LOCAL_UNROLL + g
            rows = pl.ds(g * CHUNK, CHUNK)
            q = _conv_silu(q_ref, wq_ref, n, CHUNK, n_chunks)
            k = _conv_silu(k_ref, wk_ref, n, CHUNK, n_chunks)
            qkv_ref[0, rows, :] = q * lax.rsqrt(jnp.sum(q * q, axis=-1, keepdims=True) + NORM_EPS) * (HEAD_W ** -0.5)
            qkv_ref[1, rows, :] = k * lax.rsqrt(jnp.sum(k * k, axis=-1, keepdims=True) + NORM_EPS)
            qkv_ref[2, rows, :] = _conv_silu(v_ref, wv_ref, n, CHUNK, n_chunks)

    def local_group(j, qkv_ref):
        ns = [j * LOCAL_UNROLL + g for g in range(LOCAL_UNROLL)]
        t0s = [pl.multiple_of(n * CHUNK, CHUNK) for n in ns]
        qs, ks, vs = ([qkv_ref[i, pl.ds(g * CHUNK, CHUNK), :] for g in range(LOCAL_UNROLL)] for i in range(3))
        qk2s = [_dot_nt(jnp.concatenate([q, k], axis=0), k) for q, k in zip(qs, ks)]
        chains = []
        for g, n in enumerate(ns):
            gates = gcol_ref[0, pl.ds(t0s[g], CHUNK), :]
            for d in range(2):
                sel = head + d * N_HEADS
                gc = jnp.sum(jnp.where(lane == sel, gates, 0.0), axis=1, keepdims=True)
                beta = jnp.sum(jnp.where(lane == sel + 2 * N_HEADS, gates, 0.0), axis=1, keepdims=True)
                gc_row = grow_ref[sel, 0, pl.ds(n, 1), :]
                incl, strict, last = (row >= col, row > col, CHUNK - 1) if d == 0 else (row <= col, row < col, 0)
                decay = jnp.exp(jnp.where(incl, gc - gc_row, MASKED))
                a = jnp.where(strict, qk2s[g][CHUNK:] * beta * decay, 0.0)
                chains.append((g, d, gc, beta, gc_row[:, last:last + 1], decay, a))
        ts = _unit_triangular_inverses([c[-1] for c in chains])
        egcs = [jnp.exp(c[2]) for c in chains]
        uws = [_dot(t, jnp.concatenate([vs[c[0]] * c[3], ks[c[0]] * (c[3] * egc)], axis=1))
               for t, c, egc in zip(ts, chains, egcs)]
        kdts = [(ks[c[0]] * jnp.exp(c[4] - c[2])).T for c in chains]
        trans = [_dot(kdt, uw) for kdt, uw in zip(kdts, uws)]
        for (g, d, gc, beta, gl, decay, _), egc, uw, tr in zip(chains, egcs, uws, trans):
            t0, n = t0s[g], ns[g]
            sn_ref[d, pl.ds(t0, CHUNK), :] = tr[:, :HEAD_W]
            sm_ref[d, pl.ds(t0, CHUNK), :] = tr[:, HEAD_W:].astype(BF16)
            u_ref[d, pl.ds(t0, CHUNK), :] = uw[:, :HEAD_W]
            wqg_ref[d, pl.ds(pl.multiple_of(2 * t0, CHUNK), CHUNK), :] = uw[:, HEAD_W:].astype(BF16)
            wqg_ref[d, pl.ds(pl.multiple_of(2 * t0 + CHUNK, CHUNK), CHUNK), :] = (qs[g] * egc).astype(BF16)
            attn_ref[d, pl.ds(t0, CHUNK), :] = (qk2s[g][:CHUNK] * decay).astype(BF16)
            egl_ref[d, pl.ds(n, 1), :] = jnp.broadcast_to(jnp.exp(gl), (1, LANES))

    n_groups = n_chunks // LOCAL_UNROLL
    conv_group(0, qkv0_ref)

    def local_body(i, _):
        j = 2 * i
        conv_group(j + 1, qkv1_ref)
        local_group(j, qkv0_ref)
        conv_group(jnp.minimum(j + 2, n_groups - 1), qkv0_ref)
        local_group(j + 1, qkv1_ref)
        return 0

    lax.fori_loop(0, n_groups // 2, local_body, 0)

    def chunk_starts(i):
        return [pl.multiple_of(n * CHUNK, CHUNK) for n in (i, n_chunks - 1 - i)]

    def emit_outputs(i, v_new, rq):
        t0s = chunk_starts(i)
        intra = [jnp.dot(attn_ref[d, pl.ds(t0s[d], CHUNK), :], v_new[d], preferred_element_type=F32) for d in range(2)]
        for d in range(2):
            dir_ref[d, pl.ds(t0s[d], CHUNK), :] = rq[d] + intra[d]

    def state_step(i, carry):
        states, v_prev, rq_prev = carry
        t0s = chunk_starts(i)
        s16 = [states[d].astype(BF16) for d in range(2)]
        ms = [jnp.dot(sm_ref[d, pl.ds(t0s[d], CHUNK), :], s16[d], preferred_element_type=F32) for d in range(2)]
        rs = [jnp.dot(wqg_ref[d, pl.ds(pl.multiple_of(2 * t0s[d], 2 * CHUNK), 2 * CHUNK), :],
                      s16[d], preferred_element_type=F32) for d in range(2)]
        emit_outputs(jnp.maximum(i - 1, 0), v_prev, rq_prev)
        new_states = tuple(states[d] * egl_ref[d, pl.ds(n, 1), :] - ms[d] + sn_ref[d, pl.ds(t0s[d], CHUNK), :]
                           for d, n in enumerate((i, n_chunks - 1 - i)))
        v_new = tuple((u_ref[d, pl.ds(t0s[d], CHUNK), :] - rs[d][:CHUNK]).astype(BF16) for d in range(2))
        rq = tuple(rs[d][CHUNK:] for d in range(2))
        return new_states, v_new, rq

    zero = jnp.zeros((HEAD_W, HEAD_W), F32)
    zero_c = jnp.zeros((CHUNK, HEAD_W), F32)
    init = ((zero, zero), (zero_c.astype(BF16),) * 2, (zero_c, zero_c))
    _, v_last, rq_last = lax.fori_loop(0, n_chunks, state_step, init)
    emit_outputs(n_chunks - 1, v_last, rq_last)

    def finish_body(j, _):
        t0 = pl.multiple_of(j * (LOCAL_UNROLL * CHUNK), LOCAL_UNROLL * CHUNK)
        rows = pl.ds(t0, LOCAL_UNROLL * CHUNK)
        o = dir_ref[0, rows, :] + dir_ref[1, rows, :]
        y = o * lax.rsqrt(jnp.mean(o * o, axis=-1, keepdims=True) + NORM_EPS) * gain_ref[...]
        o_ref[0, rows, :] = (y * _silu(z_ref[0, rows, :].astype(F32))).astype(o_ref.dtype)
        return 0

    lax.fori_loop(0, n_groups, finish_body, 0)


def _gdn(qkv, gcol, grow, z, conv_w, gain):
    b, s, _ = qkv.shape
    nh = N_HEADS
    n_chunks = s // CHUNK
    assert s % CHUNK == 0 and n_chunks % (2 * LOCAL_UNROLL) == 0
    seq_spec = lambda off: pl.BlockSpec((1, s, HEAD_W), lambda bi, hi: (bi, 0, hi + off))
    w_spec = lambda off: pl.BlockSpec((CONV_W, HEAD_W), lambda bi, hi: (0, hi + off))
    return pl.pallas_call(
        _gdn_kernel,
        grid=(b, nh),
        in_specs=[
            seq_spec(0), seq_spec(nh), seq_spec(2 * nh),
            pl.BlockSpec((1, s, LANES), lambda bi, hi: (bi, 0, 0)),
            pl.BlockSpec((2 * nh, 1, n_chunks, CHUNK), lambda bi, hi: (0, bi, 0, 0)),
            seq_spec(nh),
            w_spec(0), w_spec(nh), w_spec(2 * nh),
            pl.BlockSpec((1, HEAD_W), lambda bi, hi: (0, 0)),
        ],
        out_specs=seq_spec(0),
        out_shape=jax.ShapeDtypeStruct((b, s, nh * HEAD_W), BF16),
        scratch_shapes=[
            pltpu.VMEM((3, LOCAL_UNROLL * CHUNK, HEAD_W), F32),
            pltpu.VMEM((3, LOCAL_UNROLL * CHUNK, HEAD_W), F32),
            pltpu.VMEM((2, s, HEAD_W), F32),
            pltpu.VMEM((2, 2 * s, HEAD_W), BF16),
            pltpu.VMEM((2, s, HEAD_W), BF16),
            pltpu.VMEM((2, s, HEAD_W), F32),
            pltpu.VMEM((2, s, CHUNK), BF16),
            pltpu.VMEM((2, n_chunks, LANES), F32),
            pltpu.VMEM((2, s, HEAD_W), F32),
        ],
        compiler_params=_cparams(("parallel", "arbitrary")),
        name="gated_delta",
    )(qkv, qkv, qkv, gcol, grow, z, conv_w, conv_w, conv_w, gain.reshape(1, HEAD_W))


def _merge_kernel(ya_ref, yb_ref, gate_ref, x_ref, wa_ref, wb_ref, wo_ref, g_ref, *out_refs, final, tn):
    d = x_ref.shape[1]
    merged = []
    for j in range(d // tn):
        sl = slice(j * tn, (j + 1) * tn)
        pa = jnp.dot(ya_ref[...], wa_ref[:, sl], preferred_element_type=F32)
        pb = jnp.dot(yb_ref[...], wb_ref[:, sl], preferred_element_type=F32)
        ga = gate_ref[:, j * tn:(j + 1) * tn].astype(F32)
        gb = gate_ref[:, d + j * tn:d + (j + 1) * tn].astype(F32)
        merged.append((ga * pa + gb * pb).astype(BF16))
    merged = jnp.concatenate(merged, axis=1)
    x = x_ref[...] + jnp.dot(merged, wo_ref[...], preferred_element_type=F32)
    y = x * lax.rsqrt(jnp.mean(x * x, axis=-1, keepdims=True) + NORM_EPS) * g_ref[...]
    if final:
        out_refs[0][...] = y
    else:
        out_refs[0][...] = x
        out_refs[1][...] = y.astype(BF16)


def _merge(ya, yb, gates, x, wa, wb, wo, g_next, final, tm=512, tn=256):
    m, d = x.shape
    tm = min(tm, m)
    row = lambda width: pl.BlockSpec((tm, width), lambda i: (i, 0))
    full = lambda r, c: pl.BlockSpec((r, c), lambda i: (0, 0))
    if final:
        out_shape = [jax.ShapeDtypeStruct((m, d), F32)]
        out_specs = [row(d)]
    else:
        out_shape = [jax.ShapeDtypeStruct((m, d), F32), jax.ShapeDtypeStruct((m, d), BF16)]
        out_specs = [row(d), row(d)]
    return pl.pallas_call(
        functools.partial(_merge_kernel, final=final, tn=tn),
        grid=(m // tm,),
        in_specs=[row(d), row(d), row(2 * d), row(d), full(d, d), full(d, d), full(d, d), full(1, d)],
        out_specs=out_specs,
        out_shape=out_shape,
        compiler_params=_cparams(("parallel",)),
        name="merge_out",
    )(ya, yb, gates, x, wa, wb, wo, g_next.reshape(1, d))


def _rope_tables(s):
    inv = ROPE_THETA ** (-jnp.arange(0, HD_QK, 2, dtype=F32) / HD_QK)
    ang = jnp.arange(s, dtype=F32)[:, None] * inv[None, :]
    ang = jnp.concatenate([ang, ang], -1)
    sign = jnp.where(jnp.arange(HD_QK) < HD_QK // 2, -1.0, 1.0).astype(F32)
    cos = jnp.tile(jnp.cos(ang), (1, LANES // HD_QK))
    sin = jnp.tile(jnp.sin(ang) * sign[None, :], (1, LANES // HD_QK))
    return cos, sin


def _trunk(x, norm_g, w_in, conv_w, lam_qk, diff_norm_g, a_log, dt_bias, gdn_norm_g, w_branch, w_out, final_g):
    b, s, d = x.shape
    depth = w_in.shape[0]
    m = b * s
    w_a = N_HEADS * HEAD_W
    qk_w = N_HEADS * 2 * HD_QK
    c_q, c_k, c_v, c_za = 0, qk_w, 2 * qk_w, 2 * qk_w + w_a
    c_qkvb = c_za + w_a
    c_zb = c_qkvb + 3 * w_a
    c_a = c_zb + w_a
    c_b = c_a + 2 * N_HEADS
    c_gate = c_b + 2 * N_HEADS

    cos, sin = _rope_tables(s)
    tm = min(1024, s)
    pos_blocks = s // tm
    rope_specs = [pl.BlockSpec((tm, LANES), lambda i: (i % pos_blocks, 0))] * 2

    x2 = x.reshape(m, d)
    h = _rmsnorm(x2, norm_g[0], BF16)
    for l in range(depth):
        w = w_in[l]
        scale = HD_QK ** -0.5 * math.log2(math.e)
        w_qk = jnp.concatenate([w[:, c_q:c_k] * scale, w[:, c_k:c_v]], axis=1).astype(BF16)
        w_v = w[:, c_v:c_za].astype(BF16)
        w_z = jnp.concatenate([w[:, c_za:c_qkvb], w[:, c_zb:c_a]], axis=1).astype(BF16)
        w_qkvb = w[:, c_qkvb:c_zb].astype(BF16)
        w_ab = jnp.pad(w[:, c_a:c_gate], ((0, 0), (0, LANES - 4 * N_HEADS))).astype(BF16)
        w_gate = w[:, c_gate:].astype(BF16)
        alog_vec = jnp.pad(a_log[l].reshape(1, -1), ((0, 0), (0, LANES - 2 * N_HEADS)))
        dtb_vec = jnp.pad(dt_bias[l].reshape(1, -1), ((0, 0), (0, LANES - 2 * N_HEADS)))

        qk = _proj(_proj_rope_kernel, h, w_qk, (cos, sin), rope_specs, BF16, tm, "proj_qk_rope")
        v_a = _proj(_proj_plain_kernel, h, w_v, (), [], BF16, tm, "proj_v")
        z_ab = _proj(_proj_plain_kernel, h, w_z, (), [], BF16, tm, "proj_z")
        qkv_b = _proj(_proj_plain_kernel, h, w_qkvb, (), [], BF16, tm, "proj_qkv_delta")
        gcol, grow = _proj_gdn_gates(h, w_ab, alog_vec, dtb_vec, tm)
        merge_gates = _proj(_proj_sigmoid_kernel, h, w_gate, (), [], BF16, tm, "proj_merge_gates")

        lam_init = 0.8 - 0.6 * math.exp(-0.3 * l)
        lq = lam_qk[l].astype(F32)
        lam = (jnp.exp(jnp.sum(lq[0] * lq[1])) - jnp.exp(jnp.sum(lq[2] * lq[3])) + lam_init).reshape(1)

        z_ab = z_ab.reshape(b, s, 2 * w_a)
        y_a = _diff_attention(lam, qk.reshape(b, s, 2 * qk_w), v_a.reshape(b, s, w_a), z_ab,
                              diff_norm_g[l], 1.0 - lam_init)
        y_b = _gdn(qkv_b.reshape(b, s, 3 * w_a), gcol.reshape(b, s, LANES),
                   grow.reshape(2 * N_HEADS, b, s // CHUNK, CHUNK), z_ab, conv_w[l], gdn_norm_g[l])

        final = l == depth - 1
        g_next = final_g if final else norm_g[l + 1]
        outs = _merge(y_a.reshape(m, w_a), y_b.reshape(m, w_a), merge_gates, x2,
                      w_branch[l, 0].astype(BF16), w_branch[l, 1].astype(BF16), w_out[l].astype(BF16),
                      g_next, final)
        if final:
            x2 = outs[0]
        else:
            x2, h = outs
    return x2.reshape(b, s, d)


def kernel(x_prompt, x_sample, norm_g, w_in, conv_w, lam_qk, diff_norm_g, a_log, dt_bias, gdn_norm_g, w_branch, w_out, final_g):
    params = (norm_g, w_in, conv_w, lam_qk, diff_norm_g, a_log, dt_bias, gdn_norm_g, w_branch, w_out, final_g)
    return (_trunk(x_prompt, *params), _trunk(x_sample, *params))
```

```python
import functools
import math

import jax
import jax.numpy as jnp
from jax import lax
from jax.experimental import pallas as pl
from jax.experimental.pallas import tpu as pltpu

F32 = jnp.float32
BF16 = jnp.bfloat16

LANES = 128
N_HEADS = 8
HD_QK = 64
HEAD_W = 128
CONV_W = 4
CHUNK = 128
ROPE_THETA = 10000.0
NORM_EPS = 1e-6
SUBLN_EPS = 1e-5
VMEM_LIMIT = 56 * 1024 * 1024
MASKED = -1e30


def _cparams(sem):
    return pltpu.CompilerParams(dimension_semantics=sem, vmem_limit_bytes=VMEM_LIMIT)


def _silu(x):
    return x * (1.0 / (1.0 + jnp.exp(-x)))


def _sigmoid(x):
    return 1.0 / (1.0 + jnp.exp(-x))


def _rmsnorm_kernel(x_ref, g_ref, o_ref, *, eps):
    x = x_ref[...]
    y = x * lax.rsqrt(jnp.mean(x * x, axis=-1, keepdims=True) + eps)
    o_ref[...] = (y * g_ref[...]).astype(o_ref.dtype)


def _rmsnorm(x, g, out_dtype, tm=1024):
    m, d = x.shape
    tm = min(tm, m)
    return pl.pallas_call(
        functools.partial(_rmsnorm_kernel, eps=NORM_EPS),
        grid=(m // tm,),
        in_specs=[pl.BlockSpec((tm, d), lambda i: (i, 0)), pl.BlockSpec((1, d), lambda i: (0, 0))],
        out_specs=pl.BlockSpec((tm, d), lambda i: (i, 0)),
        out_shape=jax.ShapeDtypeStruct((m, d), out_dtype),
        compiler_params=_cparams(("parallel",)),
        name="rmsnorm",
    )(x, g.reshape(1, d))


PROJ_COLS = 512


def _proj_plain_kernel(h_ref, w_ref, o_ref):
    h = h_ref[...]
    for j in range(o_ref.shape[1] // PROJ_COLS):
        cols = slice(j * PROJ_COLS, (j + 1) * PROJ_COLS)
        o_ref[:, cols] = jnp.dot(h, w_ref[:, cols], preferred_element_type=F32).astype(o_ref.dtype)


def _proj_sigmoid_kernel(h_ref, w_ref, o_ref):
    h = h_ref[...]
    for j in range(o_ref.shape[1] // PROJ_COLS):
        cols = slice(j * PROJ_COLS, (j + 1) * PROJ_COLS)
        acc = jnp.dot(h, w_ref[:, cols], preferred_element_type=F32)
        o_ref[:, cols] = _sigmoid(acc).astype(o_ref.dtype)


def _proj_rope_kernel(h_ref, w_ref, cos_ref, sin_ref, o_ref):
    h = h_ref[...]
    cos = cos_ref[...]
    sin = sin_ref[...]
    lane = lax.broadcasted_iota(jnp.int32, cos.shape, 1)
    first_half = (lane % HD_QK) < (HD_QK // 2)
    for j in range(o_ref.shape[1] // PROJ_COLS):
        acc = jnp.dot(h, w_ref[:, j * PROJ_COLS:(j + 1) * PROJ_COLS], preferred_element_type=F32)
        for g in range(PROJ_COLS // LANES):
            a = acc[:, g * LANES:(g + 1) * LANES]
            rot = jnp.where(first_half, pltpu.roll(a, LANES - HD_QK // 2, 1), pltpu.roll(a, HD_QK // 2, 1))
            lanes = slice(j * PROJ_COLS + g * LANES, j * PROJ_COLS + (g + 1) * LANES)
            o_ref[:, lanes] = (a * cos + rot * sin).astype(o_ref.dtype)


def _proj(kernel_fn, h, w, extras, extra_specs, out_dtype, tm, name):
    m, d = h.shape
    n = w.shape[1]
    return pl.pallas_call(
        kernel_fn,
        grid=(m // tm,),
        in_specs=[pl.BlockSpec((tm, d), lambda i: (i, 0)), pl.BlockSpec((d, n), lambda i: (0, 0))] + extra_specs,
        out_specs=pl.BlockSpec((tm, n), lambda i: (i, 0)),
        out_shape=jax.ShapeDtypeStruct((m, n), out_dtype),
        compiler_params=_cparams(("parallel",)),
        name=name,
    )(h, w, *extras)


def _proj_gdn_gate_kernel(h_ref, w_ref, alog_ref, dtb_ref, col_ref, row_ref):
    acc = jnp.dot(h_ref[...], w_ref[...], preferred_element_type=F32)
    lane = lax.broadcasted_iota(jnp.int32, acc.shape, 1)
    x = acc + dtb_ref[...]
    softplus = jnp.maximum(x, 0.0) + jnp.log(1.0 + jnp.exp(-jnp.abs(x)))
    g = jnp.where(lane < 2 * N_HEADS, -jnp.exp(alog_ref[...]) * softplus, 0.0)
    beta = _sigmoid(acc)
    r = lax.broadcasted_iota(jnp.int32, (CHUNK, CHUNK), 0)
    c = lax.broadcasted_iota(jnp.int32, (CHUNK, CHUNK), 1)
    tril = jnp.where(r >= c, 1.0, 0.0).astype(F32)
    lane_c = lax.broadcasted_iota(jnp.int32, (CHUNK, LANES), 1)
    for ci in range(acc.shape[0] // CHUNK):
        rows = slice(ci * CHUNK, (ci + 1) * CHUNK)
        g_c = g[rows]
        prefix = jnp.dot(tril, g_c, preferred_element_type=F32, precision=lax.Precision.HIGHEST)
        suffix = prefix[CHUNK - 1:CHUNK, :] - prefix + g_c
        out = jnp.where(lane_c < N_HEADS, prefix, jnp.where(lane_c < 2 * N_HEADS, suffix, beta[rows]))
        col_ref[rows, :] = out
        row_ref[:, rows] = out.T[:2 * N_HEADS, :]


def _proj_gdn_gates(h, w, alog_vec, dtb_vec, tm):
    m, d = h.shape
    vec = pl.BlockSpec((1, LANES), lambda i: (0, 0))
    return pl.pallas_call(
        _proj_gdn_gate_kernel,
        grid=(m // tm,),
        in_specs=[pl.BlockSpec((tm, d), lambda i: (i, 0)), pl.BlockSpec((d, LANES), lambda i: (0, 0)), vec, vec],
        out_specs=[pl.BlockSpec((tm, LANES), lambda i: (i, 0)), pl.BlockSpec((2 * N_HEADS, tm), lambda i: (0, i))],
        out_shape=[jax.ShapeDtypeStruct((m, LANES), F32), jax.ShapeDtypeStruct((2 * N_HEADS, m), F32)],
        compiler_params=_cparams(("parallel",)),
        name="proj_delta_gates",
    )(h, w, alog_vec, dtb_vec)


ATTN_ROW_BLOCKS = 4


def _attn_kernel(lam_ref, q_ref, k_ref, v_ref, z_ref, g_ref, o_ref, s0_ref, s1_ref, acc_ref, m_ref, *, tk, post_scale):
    tq = q_ref.shape[1]
    s_len = k_ref.shape[1]
    nk = s_len // tk
    nrb = ATTN_ROW_BLOCKS
    rb = 2 * tq // nrb
    q = q_ref[0]
    lane = lax.broadcasted_iota(jnp.int32, q.shape, 1)
    zero = jnp.zeros_like(q)
    q2 = jnp.concatenate([jnp.where(lane < HD_QK, q, zero), jnp.where(lane >= HD_QK, q, zero)], axis=0)
    ones = jnp.ones((tk, HEAD_W), BF16)

    def scores(kt, s_ref, r):
        rows = slice(r * rb, (r + 1) * rb)
        k = k_ref[0, pl.ds(pl.multiple_of(kt * tk, tk), tk), :]
        s_ref[rows, :] = lax.dot_general(q2[rows], k, (((1,), (1,)), ((), ())), preferred_element_type=F32)

    def update(kt, s_ref, r):
        rows = slice(r * rb, (r + 1) * rb)
        s = s_ref[rows, :]
        m_prev = m_ref[rows, :]
        m_new = jnp.maximum(m_prev, jnp.max(s, axis=1, keepdims=True))
        alpha = jnp.exp2(m_prev - m_new)
        p = jnp.exp2(s - m_new).astype(BF16)
        v_aug = jnp.concatenate([v_ref[0, pl.ds(pl.multiple_of(kt * tk, tk), tk), :], ones], axis=1)
        acc_ref[rows, :] = alpha * acc_ref[rows, :] + jnp.dot(p, v_aug, preferred_element_type=F32)
        m_ref[rows, :] = m_new

    def step(kt_next, s_next, kt, s_cur):
        lead = 0
        if kt_next is not None:
            for r in range(lead):
                scores(kt_next, s_next, r)
        for r in range(nrb):
            if kt_next is not None and r + lead < nrb:
                scores(kt_next, s_next, r + lead)
            update(kt, s_cur, r)

    m_ref[...] = jnp.full(m_ref.shape, -jnp.inf, F32)
    acc_ref[...] = jnp.zeros(acc_ref.shape, F32)
    for r in range(nrb):
        scores(0, s0_ref, r)

    def body(j, _):
        kt = 2 * j
        step(kt + 1, s1_ref, kt, s0_ref)
        step(kt + 2, s0_ref, kt + 1, s1_ref)
        return 0

    lax.fori_loop(0, nk // 2 - 1, body, 0)
    step(nk - 1, s1_ref, nk - 2, s0_ref)
    step(None, None, nk - 1, s1_ref)
    acc = acc_ref[...]
    o2 = acc[:, :HEAD_W] / acc[:, HEAD_W:]
    o = o2[:tq] - lam_ref[0] * o2[tq:]
    y = o * lax.rsqrt(jnp.mean(o * o, axis=-1, keepdims=True) + SUBLN_EPS) * g_ref[...] * post_scale
    o_ref[0] = (y * _silu(z_ref[0].astype(F32))).astype(o_ref.dtype)


def _diff_attention(lam, qk, v, z, gain, post_scale, tq=512, tk=2048):
    b, s, w = v.shape
    nh = w // HEAD_W
    tq = min(tq, s)
    tk = min(tk, s // 2)
    assert s % (2 * tk) == 0 and s % tq == 0
    return pl.pallas_call(
        functools.partial(_attn_kernel, tk=tk, post_scale=post_scale),
        scratch_shapes=[
            pltpu.VMEM((2 * tq, tk), F32),
            pltpu.VMEM((2 * tq, tk), F32),
            pltpu.VMEM((2 * tq, 2 * HEAD_W), F32),
            pltpu.VMEM((2 * tq, 1), F32),
        ],
        grid=(b, nh, s // tq),
        in_specs=[
            pl.BlockSpec(memory_space=pltpu.SMEM),
            pl.BlockSpec((1, tq, HEAD_W), lambda bi, hi, qi: (bi, qi, hi)),
            pl.BlockSpec((1, s, HEAD_W), lambda bi, hi, qi: (bi, 0, nh + hi)),
            pl.BlockSpec((1, s, HEAD_W), lambda bi, hi, qi: (bi, 0, hi)),
            pl.BlockSpec((1, tq, HEAD_W), lambda bi, hi, qi: (bi, qi, hi)),
            pl.BlockSpec((1, HEAD_W), lambda bi, hi, qi: (0, 0)),
        ],
        out_specs=pl.BlockSpec((1, tq, HEAD_W), lambda bi, hi, qi: (bi, qi, hi)),
        out_shape=jax.ShapeDtypeStruct((b, s, w), BF16),
        compiler_params=_cparams(("parallel", "parallel", "arbitrary")),
        name="diff_attention",
    )(lam, qk, qk, v, z, gain.reshape(1, HEAD_W))


def _dot(a, b):
    return jnp.dot(a.astype(BF16), b.astype(BF16), preferred_element_type=F32)


def _dot_nt(a, b):
    return lax.dot_general(a.astype(BF16), b.astype(BF16), (((1,), (1,)), ((), ())), preferred_element_type=F32)


INV_BLOCK = 64
LOCAL_UNROLL = 4


def _unit_triangular_inverses(mats):
    n = mats[0].shape[0]
    row = lax.broadcasted_iota(jnp.int32, (n, n), 0)
    col = lax.broadcasted_iota(jnp.int32, (n, n), 1)
    same_block = (row // INV_BLOCK) == (col // INV_BLOCK)
    eye = jnp.where(row == col, 1.0, 0.0).astype(F32)
    a_diag = [jnp.where(same_block, a, 0.0) for a in mats]
    a_off = [jnp.where(same_block, 0.0, a) for a in mats]
    def mm(a16, b16):
        return jnp.dot(a16, b16, preferred_element_type=F32)

    ts = [eye - a for a in a_diag]
    a16 = [a.astype(BF16) for a in a_diag]
    qs = [mm(a, a) for a in a16]
    levels = int(math.log2(INV_BLOCK)) - 1
    for level in range(levels):
        q16 = [q.astype(BF16) for q in qs]
        t16 = [t.astype(BF16) for t in ts]
        if level < levels - 1:
            prods = [mm(q, jnp.concatenate([q, t], axis=1)) for q, t in zip(q16, t16)]
            qs = [p[:, :n] for p in prods]
            ts = [t + p[:, n:] for t, p in zip(ts, prods)]
        else:
            prods = [mm(q, t) for q, t in zip(q16, t16)]
            ts = [t + p for t, p in zip(ts, prods)]
    t16 = [t.astype(BF16) for t in ts]
    offs = [mm(a.astype(BF16), t) for a, t in zip(a_off, t16)]
    corr = [mm(t, o.astype(BF16)) for t, o in zip(t16, offs)]
    return [t - c for t, c in zip(ts, corr)]


def _conv_silu(x_ref, w_ref, blk, rows, n_blk):
    halo = 16
    t0 = pl.multiple_of(blk * rows, rows)
    xc = x_ref[0, pl.ds(t0, rows), :].astype(F32)
    prev_start = pl.multiple_of(jnp.maximum(t0 - halo, 0), halo)
    next_start = pl.multiple_of(jnp.minimum(t0 + rows, n_blk * rows - halo), halo)
    prev = jnp.where(blk == 0, 0.0, x_ref[0, pl.ds(prev_start, halo), :].astype(F32))
    nxt = jnp.where(blk == n_blk - 1, 0.0, x_ref[0, pl.ds(next_start, halo), :].astype(F32))
    xx = jnp.concatenate([prev, xc, nxt], axis=0)
    n = rows + 2 * halo
    x_m1 = pltpu.roll(xx, 1, 0)[halo:halo + rows]
    x_p1 = pltpu.roll(xx, n - 1, 0)[halo:halo + rows]
    x_p2 = pltpu.roll(xx, n - 2, 0)[halo:halo + rows]
    w = w_ref[...]
    y = x_m1 * w[0:1, :] + xc * w[1:2, :] + x_p1 * w[2:3, :] + x_p2 * w[3:4, :]
    return _silu(y)


def _gdn_kernel(q_ref, k_ref, v_ref, gcol_ref, grow_ref, z_ref, wq_ref, wk_ref, wv_ref, gain_ref, o_ref,
                qkv0_ref, qkv1_ref, u_ref, wqg_ref, sm_ref, sn_ref, attn_ref, egl_ref, dir_ref):
    s_len = q_ref.shape[1]
    n_chunks = s_len // CHUNK
    head = pl.program_id(1)
    row = lax.broadcasted_iota(jnp.int32, (CHUNK, CHUNK), 0)
    col = lax.broadcasted_iota(jnp.int32, (CHUNK, CHUNK), 1)
    lane = lax.broadcasted_iota(jnp.int32, (CHUNK, LANES), 1)

    def conv_group(j, qkv_ref):
        for g in range(LOCAL_UNROLL):
            n = j * LOCAL_UNROLL + g
            rows = pl.ds(g * CHUNK, CHUNK)
            q = _conv_silu(q_ref, wq_ref, n, CHUNK, n_chunks)
            k = _conv_silu(k_ref, wk_ref, n, CHUNK, n_chunks)
            qkv_ref[0, rows, :] = q * lax.rsqrt(jnp.sum(q * q, axis=-1, keepdims=True) + NORM_EPS) * (HEAD_W ** -0.5)
            qkv_ref[1, rows, :] = k * lax.rsqrt(jnp.sum(k * k, axis=-1, keepdims=True) + NORM_EPS)
            qkv_ref[2, rows, :] = _conv_silu(v_ref, wv_ref, n, CHUNK, n_chunks)

    def local_group(j, qkv_ref):
        ns = [j * LOCAL_UNROLL + g for g in range(LOCAL_UNROLL)]
        t0s = [pl.multiple_of(n * CHUNK, CHUNK) for n in ns]
        qs, ks, vs = ([qkv_ref[i, pl.ds(g * CHUNK, CHUNK), :] for g in range(LOCAL_UNROLL)] for i in range(3))
        qk2s = [_dot_nt(jnp.concatenate([q, k], axis=0), k) for q, k in zip(qs, ks)]
        chains = []
        for g, n in enumerate(ns):
            gates = gcol_ref[0, pl.ds(t0s[g], CHUNK), :]
            for d in range(2):
                sel = head + d * N_HEADS
                gc = jnp.sum(jnp.where(lane == sel, gates, 0.0), axis=1, keepdims=True)
                beta = jnp.sum(jnp.where(lane == sel + 2 * N_HEADS, gates, 0.0), axis=1, keepdims=True)
                gc_row = grow_ref[sel, 0, pl.ds(n, 1), :]
                incl, strict, last = (row >= col, row > col, CHUNK - 1) if d == 0 else (row <= col, row < col, 0)
                decay = jnp.exp(jnp.where(incl, gc - gc_row, MASKED))
                a = jnp.where(strict, qk2s[g][CHUNK:] * beta * decay, 0.0)
                chains.append((g, d, gc, beta, gc_row[:, last:last + 1], decay, a))
        ts = _unit_triangular_inverses([c[-1] for c in chains])
        egcs = [jnp.exp(c[2]) for c in chains]
        uws = [_dot(t, jnp.concatenate([vs[c[0]] * c[3], ks[c[0]] * (c[3] * egc)], axis=1))
               for t, c, egc in zip(ts, chains, egcs)]
        kdts = [(ks[c[0]] * jnp.exp(c[4] - c[2])).T for c in chains]
        trans = [_dot(kdt, uw) for kdt, uw in zip(kdts, uws)]
        for (g, d, gc, beta, gl, decay, _), egc, uw, tr in zip(chains, egcs, uws, trans):
            t0, n = t0s[g], ns[g]
            sn_ref[d, pl.ds(t0, CHUNK), :] = tr[:, :HEAD_W]
            sm_ref[d, pl.ds(t0, CHUNK), :] = tr[:, HEAD_W:].astype(BF16)
            u_ref[d, pl.ds(t0, CHUNK), :] = uw[:, :HEAD_W]
            wqg_ref[d, pl.ds(pl.multiple_of(2 * t0, CHUNK), CHUNK), :] = uw[:, HEAD_W:].astype(BF16)
            wqg_ref[d, pl.ds(pl.multiple_of(2 * t0 + CHUNK, CHUNK), CHUNK), :] = (qs[g] * egc).astype(BF16)
            attn_ref[d, pl.ds(t0, CHUNK), :] = (qk2s[g][:CHUNK] * decay).astype(BF16)
            egl_ref[d, pl.ds(n, 1), :] = jnp.broadcast_to(jnp.exp(gl), (1, LANES))

    n_groups = n_chunks // LOCAL_UNROLL
    conv_group(0, qkv0_ref)

    def local_body(i, _):
        j = 2 * i
        conv_group(j + 1, qkv1_ref)
        local_group(j, qkv0_ref)
        conv_group(jnp.minimum(j + 2, n_groups - 1), qkv0_ref)
        local_group(j + 1, qkv1_ref)
        return 0

    lax.fori_loop(0, n_groups // 2, local_body, 0)

    def chunk_starts(i):
        return [pl.multiple_of(n * CHUNK, CHUNK) for n in (i, n_chunks - 1 - i)]

    def emit_outputs(i, v_new, rq):
        t0s = chunk_starts(i)
        intra = [jnp.dot(attn_ref[d, pl.ds(t0s[d], CHUNK), :], v_new[d], preferred_element_type=F32) for d in range(2)]
        for d in range(2):
            dir_ref[d, pl.ds(t0s[d], CHUNK), :] = rq[d] + intra[d]

    def state_step(i, carry):
        states, v_prev, rq_prev = carry
        t0s = chunk_starts(i)
        s16 = [states[d].astype(BF16) for d in range(2)]
        ms = [jnp.dot(sm_ref[d, pl.ds(t0s[d], CHUNK), :], s16[d], preferred_element_type=F32) for d in range(2)]
        rs = [jnp.dot(wqg_ref[d, pl.ds(pl.multiple_of(2 * t0s[d], 2 * CHUNK), 2 * CHUNK), :],
                      s16[d], preferred_element_type=F32) for d in range(2)]
        emit_outputs(jnp.maximum(i - 1, 0), v_prev, rq_prev)
        new_states = tuple(states[d] * egl_ref[d, pl.ds(n, 1), :] - ms[d] + sn_ref[d, pl.ds(t0s[d], CHUNK), :]
                           for d, n in enumerate((i, n_chunks - 1 - i)))
        v_new = tuple((u_ref[d, pl.ds(t0s[d], CHUNK), :] - rs[d][:CHUNK]).astype(BF16) for d in range(2))
        rq = tuple(rs[d][CHUNK:] for d in range(2))
        return new_states, v_new, rq

    zero = jnp.zeros((HEAD_W, HEAD_W), F32)
    zero_c = jnp.zeros((CHUNK, HEAD_W), F32)
    init = ((zero, zero), (zero_c.astype(BF16),) * 2, (zero_c, zero_c))
    _, v_last, rq_last = lax.fori_loop(0, n_chunks, state_step, init)
    emit_outputs(n_chunks - 1, v_last, rq_last)

    def finish_body(j, _):
        t0 = pl.multiple_of(j * (LOCAL_UNROLL * CHUNK), LOCAL_UNROLL * CHUNK)
        rows = pl.ds(t0, LOCAL_UNROLL * CHUNK)
        o = dir_ref[0, rows, :] + dir_ref[1, rows, :]
        y = o * lax.rsqrt(jnp.mean(o * o, axis=-1, keepdims=True) + NORM_EPS) * gain_ref[...]
        o_ref[0, rows, :] = (y * _silu(z_ref[0, rows, :].astype(F32))).astype(o_ref.dtype)
        return 0

    lax.fori_loop(0, n_groups, finish_body, 0)


def _gdn(qkv, gcol, grow, z, conv_w, gain):
    b, s, _ = qkv.shape
    nh = N_HEADS
    n_chunks = s // CHUNK
    assert s % CHUNK == 0 and n_chunks % (2 * LOCAL_UNROLL) == 0
    seq_spec = lambda off: pl.BlockSpec((1, s, HEAD_W), lambda bi, hi: (bi, 0, hi + off))
    w_spec = lambda off: pl.BlockSpec((CONV_W, HEAD_W), lambda bi, hi: (0, hi + off))
    return pl.pallas_call(
        _gdn_kernel,
        grid=(b, nh),
        in_specs=[
            seq_spec(0), seq_spec(nh), seq_spec(2 * nh),
            pl.BlockSpec((1, s, LANES), lambda bi, hi: (bi, 0, 0)),
            pl.BlockSpec((2 * nh, 1, n_chunks, CHUNK), lambda bi, hi: (0, bi, 0, 0)),
            seq_spec(nh),
            w_spec(0), w_spec(nh), w_spec(2 * nh),
            pl.BlockSpec((1, HEAD_W), lambda bi, hi: (0, 0)),
        ],
        out_specs=seq_spec(0),
        out_shape=jax.ShapeDtypeStruct((b, s, nh * HEAD_W), BF16),
        scratch_shapes=[
            pltpu.VMEM((3, LOCAL_UNROLL * CHUNK, HEAD_W), F32),
            pltpu.VMEM((3, LOCAL_UNROLL * CHUNK, HEAD_W), F32),
            pltpu.VMEM((2, s, HEAD_W), F32),
            pltpu.VMEM((2, 2 * s, HEAD_W), BF16),
            pltpu.VMEM((2, s, HEAD_W), BF16),
            pltpu.VMEM((2, s, HEAD_W), F32),
            pltpu.VMEM((2, s, CHUNK), BF16),
            pltpu.VMEM((2, n_chunks, LANES), F32),
            pltpu.VMEM((2, s, HEAD_W), F32),
        ],
        compiler_params=_cparams(("parallel", "arbitrary")),
        name="gated_delta",
    )(qkv, qkv, qkv, gcol, grow, z, conv_w, conv_w, conv_w, gain.reshape(1, HEAD_W))


def _merge_kernel(ya_ref, yb_ref, gate_ref, x_ref, wa_ref, wb_ref, wo_ref, g_ref, *out_refs, final, tn):
    d = x_ref.shape[1]
    merged = []
    for j in range(d // tn):
        sl = slice(j * tn, (j + 1) * tn)
        pa = jnp.dot(ya_ref[...], wa_ref[:, sl], preferred_element_type=F32)
        pb = jnp.dot(yb_ref[...], wb_ref[:, sl], preferred_element_type=F32)
        ga = gate_ref[:, j * tn:(j + 1) * tn].astype(F32)
        gb = gate_ref[:, d + j * tn:d + (j + 1) * tn].astype(F32)
        merged.append((ga * pa + gb * pb).astype(BF16))
    merged = jnp.concatenate(merged, axis=1)
    x = x_ref[...] + jnp.dot(merged, wo_ref[...], preferred_element_type=F32)
    y = x * lax.rsqrt(jnp.mean(x * x, axis=-1, keepdims=True) + NORM_EPS) * g_ref[...]
    if final:
        out_refs[0][...] = y
    else:
        out_refs[0][...] = x
        out_refs[1][...] = y.astype(BF16)


def _merge(ya, yb, gates, x, wa, wb, wo, g_next, final, tm=512, tn=256):
    m, d = x.shape
    tm = min(tm, m)
    row = lambda width: pl.BlockSpec((tm, width), lambda i: (i, 0))
    full = lambda r, c: pl.BlockSpec((r, c), lambda i: (0, 0))
    if final:
        out_shape = [jax.ShapeDtypeStruct((m, d), F32)]
        out_specs = [row(d)]
    else:
        out_shape = [jax.ShapeDtypeStruct((m, d), F32), jax.ShapeDtypeStruct((m, d), BF16)]
        out_specs = [row(d), row(d)]
    return pl.pallas_call(
        functools.partial(_merge_kernel, final=final, tn=tn),
        grid=(m // tm,),
        in_specs=[row(d), row(d), row(2 * d), row(d), full(d, d), full(d, d), full(d, d), full(1, d)],
        out_specs=out_specs,
        out_shape=out_shape,
        compiler_params=_cparams(("parallel",)),
        name="merge_out",
    )(ya, yb, gates, x, wa, wb, wo, g_next.reshape(1, d))


def _rope_tables(s):
    inv = ROPE_THETA ** (-jnp.arange(0, HD_QK, 2, dtype=F32) / HD_QK)
    ang = jnp.arange(s, dtype=F32)[:, None] * inv[None, :]
    ang = jnp.concatenate([ang, ang], -1)
    sign = jnp.where(jnp.arange(HD_QK) < HD_QK // 2, -1.0, 1.0).astype(F32)
    cos = jnp.tile(jnp.cos(ang), (1, LANES // HD_QK))
    sin = jnp.tile(jnp.sin(ang) * sign[None, :], (1, LANES // HD_QK))
    return cos, sin


def _trunk(x, norm_g, w_in, conv_w, lam_qk, diff_norm_g, a_log, dt_bias, gdn_norm_g, w_branch, w_out, final_g):
    b, s, d = x.shape
    depth = w_in.shape[0]
    m = b * s
    w_a = N_HEADS * HEAD_W
    qk_w = N_HEADS * 2 * HD_QK
    c_q, c_k, c_v, c_za = 0, qk_w, 2 * qk_w, 2 * qk_w + w_a
    c_qkvb = c_za + w_a
    c_zb = c_qkvb + 3 * w_a
    c_a = c_zb + w_a
    c_b = c_a + 2 * N_HEADS
    c_gate = c_b + 2 * N_HEADS

    cos, sin = _rope_tables(s)
    tm = min(1024, s)
    pos_blocks = s // tm
    rope_specs = [pl.BlockSpec((tm, LANES), lambda i: (i % pos_blocks, 0))] * 2

    x2 = x.reshape(m, d)
    h = _rmsnorm(x2, norm_g[0], BF16)
    for l in range(depth):
        w = w_in[l]
        scale = HD_QK ** -0.5 * math.log2(math.e)
        w_qk = jnp.concatenate([w[:, c_q:c_k] * scale, w[:, c_k:c_v]], axis=1).astype(BF16)
        w_v = w[:, c_v:c_za].astype(BF16)
        w_z = jnp.concatenate([w[:, c_za:c_qkvb], w[:, c_zb:c_a]], axis=1).astype(BF16)
        w_qkvb = w[:, c_qkvb:c_zb].astype(BF16)
        w_ab = jnp.pad(w[:, c_a:c_gate], ((0, 0), (0, LANES - 4 * N_HEADS))).astype(BF16)
        w_gate = w[:, c_gate:].astype(BF16)
        alog_vec = jnp.pad(a_log[l].reshape(1, -1), ((0, 0), (0, LANES - 2 * N_HEADS)))
        dtb_vec = jnp.pad(dt_bias[l].reshape(1, -1), ((0, 0), (0, LANES - 2 * N_HEADS)))

        qk = _proj(_proj_rope_kernel, h, w_qk, (cos, sin), rope_specs, BF16, tm, "proj_qk_rope")
        v_a = _proj(_proj_plain_kernel, h, w_v, (), [], BF16, tm, "proj_v")
        z_ab = _proj(_proj_plain_kernel, h, w_z, (), [], BF16, tm, "proj_z")
        qkv_b = _proj(_proj_plain_kernel, h, w_qkvb, (), [], BF16, tm, "proj_qkv_delta")
        gcol, grow = _proj_gdn_gates(h, w_ab, alog_vec, dtb_vec, tm)
        merge_gates = _proj(_proj_sigmoid_kernel, h, w_gate, (), [], BF16, tm, "proj_merge_gates")

        lam_init = 0.8 - 0.6 * math.exp(-0.3 * l)
        lq = lam_qk[l].astype(F32)
        lam = (jnp.exp(jnp.sum(lq[0] * lq[1])) - jnp.exp(jnp.sum(lq[2] * lq[3])) + lam_init).reshape(1)

        z_ab = z_ab.reshape(b, s, 2 * w_a)
        y_a = _diff_attention(lam, qk.reshape(b, s, 2 * qk_w), v_a.reshape(b, s, w_a), z_ab,
                              diff_norm_g[l], 1.0 - lam_init)
        y_b = _gdn(qkv_b.reshape(b, s, 3 * w_a), gcol.reshape(b, s, LANES),
                   grow.reshape(2 * N_HEADS, b, s // CHUNK, CHUNK), z_ab, conv_w[l], gdn_norm_g[l])

        final = l == depth - 1
        g_next = final_g if final else norm_g[l + 1]
        outs = _merge(y_a.reshape(m, w_a), y_b.reshape(m, w_a), merge_gates, x2,
                      w_branch[l, 0].astype(BF16), w_branch[l, 1].astype(BF16), w_out[l].astype(BF16),
                      g_next, final)
        if final:
            x2 = outs[0]
        else:
            x2, h = outs
    return x2.reshape(b, s, d)


def kernel(x_prompt, x_sample, norm_g, w_in, conv_w, lam_qk, diff_norm_g, a_log, dt_bias, gdn_norm_g, w_branch, w_out, final_g):
    params = (norm_g, w_in, conv_w, lam_qk, diff_norm_g, a_log, dt_bias, gdn_norm_g, w_branch, w_out, final_g)
    return (_trunk(x_prompt, *params), _trunk(x_sample, *params))
```

```python
import functools
import math

import jax
import jax.numpy as jnp
from jax import lax
from jax.experimental import pallas as pl
from jax.experimental.pallas import tpu as pltpu

F32 = jnp.float32
BF16 = jnp.bfloat16

LANES = 128
N_HEADS = 8
HD_QK = 64
HEAD_W = 128
CONV_W = 4
CHUNK = 128
ROPE_THETA = 10000.0
NORM_EPS = 1e-6
SUBLN_EPS = 1e-5
VMEM_LIMIT = 56 * 1024 * 1024
MASKED = -1e30


def _cparams(sem):
    return pltpu.CompilerParams(dimension_semantics=sem, vmem_limit_bytes=VMEM_LIMIT)


def _sigmoid(x):
    return 0.5 * jnp.tanh(0.5 * x) + 0.5


def _silu(x):
    h = 0.5 * x
    return h * jnp.tanh(h) + h


def _rmsnorm_kernel(x_ref, g_ref, o_ref, *, eps):
    x = x_ref[...]
    y = x * lax.rsqrt(jnp.mean(x * x, axis=-1, keepdims=True) + eps)
    o_ref[...] = (y * g_ref[...]).astype(o_ref.dtype)


def _rmsnorm(x, g, out_dtype, tm=1024):
    m, d = x.shape
    tm = min(tm, m)
    return pl.pallas_call(
        functools.partial(_rmsnorm_kernel, eps=NORM_EPS),
        grid=(m // tm,),
        in_specs=[pl.BlockSpec((tm, d), lambda i: (i, 0)), pl.BlockSpec((1, d), lambda i: (0, 0))],
        out_specs=pl.BlockSpec((tm, d), lambda i: (i, 0)),
        out_shape=jax.ShapeDtypeStruct((m, d), out_dtype),
        compiler_params=_cparams(("parallel",)),
        name="rmsnorm",
    )(x, g.reshape(1, d))


PROJ_COLS = 512


def _proj_plain_kernel(h_ref, w_ref, o_ref):
    h = h_ref[...]
    for j in range(o_ref.shape[1] // PROJ_COLS):
        cols = slice(j * PROJ_COLS, (j + 1) * PROJ_COLS)
        o_ref[:, cols] = jnp.dot(h, w_ref[:, cols], preferred_element_type=F32).astype(o_ref.dtype)


def _proj_sigmoid_kernel(h_ref, w_ref, o_ref):
    h = h_ref[...]
    for j in range(o_ref.shape[1] // PROJ_COLS):
        cols = slice(j * PROJ_COLS, (j + 1) * PROJ_COLS)
        acc = jnp.dot(h, w_ref[:, cols], preferred_element_type=F32)
        o_ref[:, cols] = _sigmoid(acc).astype(o_ref.dtype)


def _proj_rope_kernel(h_ref, w_ref, cos_ref, sin_ref, o_ref):
    h = h_ref[...]
    cos = cos_ref[...]
    sin = sin_ref[...]
    lane = lax.broadcasted_iota(jnp.int32, cos.shape, 1)
    first_half = (lane % HD_QK) < (HD_QK // 2)
    for j in range(o_ref.shape[1] // PROJ_COLS):
        acc = jnp.dot(h, w_ref[:, j * PROJ_COLS:(j + 1) * PROJ_COLS], preferred_element_type=F32)
        for g in range(PROJ_COLS // LANES):
            a = acc[:, g * LANES:(g + 1) * LANES]
            rot = jnp.where(first_half, pltpu.roll(a, LANES - HD_QK // 2, 1), pltpu.roll(a, HD_QK // 2, 1))
            lanes = slice(j * PROJ_COLS + g * LANES, j * PROJ_COLS + (g + 1) * LANES)
            o_ref[:, lanes] = (a * cos + rot * sin).astype(o_ref.dtype)


def _proj(kernel_fn, h, w, extras, extra_specs, out_dtype, tm, name):
    m, d = h.shape
    n = w.shape[1]
    return pl.pallas_call(
        kernel_fn,
        grid=(m // tm,),
        in_specs=[pl.BlockSpec((tm, d), lambda i: (i, 0)), pl.BlockSpec((d, n), lambda i: (0, 0))] + extra_specs,
        out_specs=pl.BlockSpec((tm, n), lambda i: (i, 0)),
        out_shape=jax.ShapeDtypeStruct((m, n), out_dtype),
        compiler_params=_cparams(("parallel",)),
        name=name,
    )(h, w, *extras)


def _proj_gdn_gate_kernel(h_ref, w_ref, alog_ref, dtb_ref, col_ref, row_ref):
    acc = jnp.dot(h_ref[...], w_ref[...], preferred_element_type=F32)
    lane = lax.broadcasted_iota(jnp.int32, acc.shape, 1)
    x = acc + dtb_ref[...]
    softplus = jnp.maximum(x, 0.0) + jnp.log(1.0 + jnp.exp(-jnp.abs(x)))
    g = jnp.where(lane < 2 * N_HEADS, (-math.log2(math.e)) * jnp.exp(alog_ref[...]) * softplus, 0.0)
    beta = _sigmoid(acc)
    r = lax.broadcasted_iota(jnp.int32, (CHUNK, CHUNK), 0)
    c = lax.broadcasted_iota(jnp.int32, (CHUNK, CHUNK), 1)
    tril = jnp.where(r >= c, 1.0, 0.0).astype(F32)
    lane_c = lax.broadcasted_iota(jnp.int32, (CHUNK, LANES), 1)
    for ci in range(acc.shape[0] // CHUNK):
        rows = slice(ci * CHUNK, (ci + 1) * CHUNK)
        g_c = g[rows]
        prefix = jnp.dot(tril, g_c, preferred_element_type=F32, precision=lax.Precision.HIGHEST)
        suffix = prefix[CHUNK - 1:CHUNK, :] - prefix + g_c
        out = jnp.where(lane_c < N_HEADS, prefix, jnp.where(lane_c < 2 * N_HEADS, suffix, beta[rows]))
        col_ref[rows, :] = out
        row_ref[:, rows] = out.T[:2 * N_HEADS, :]


def _proj_gdn_gates(h, w, alog_vec, dtb_vec, tm):
    m, d = h.shape
    vec = pl.BlockSpec((1, LANES), lambda i: (0, 0))
    return pl.pallas_call(
        _proj_gdn_gate_kernel,
        grid=(m // tm,),
        in_specs=[pl.BlockSpec((tm, d), lambda i: (i, 0)), pl.BlockSpec((d, LANES), lambda i: (0, 0)), vec, vec],
        out_specs=[pl.BlockSpec((tm, LANES), lambda i: (i, 0)), pl.BlockSpec((2 * N_HEADS, tm), lambda i: (0, i))],
        out_shape=[jax.ShapeDtypeStruct((m, LANES), F32), jax.ShapeDtypeStruct((2 * N_HEADS, m), F32)],
        compiler_params=_cparams(("parallel",)),
        name="proj_delta_gates",
    )(h, w, alog_vec, dtb_vec)


ATTN_ROW_BLOCKS = 4


def _attn_kernel(lam_ref, q_ref, k_ref, v_ref, z_ref, g_ref, o_ref, s0_ref, s1_ref, acc_ref, m_ref, *, tk, post_scale):
    tq = q_ref.shape[1]
    s_len = k_ref.shape[1]
    nk = s_len // tk
    nrb = ATTN_ROW_BLOCKS
    rb = 2 * tq // nrb
    q = q_ref[0]
    lane = lax.broadcasted_iota(jnp.int32, q.shape, 1)
    zero = jnp.zeros_like(q)
    q2 = jnp.concatenate([jnp.where(lane < HD_QK, q, zero), jnp.where(lane >= HD_QK, q, zero)], axis=0)
    ones = jnp.ones((tk, HEAD_W), BF16)

    def scores(kt, s_ref, r):
        rows = slice(r * rb, (r + 1) * rb)
        k = k_ref[0, pl.ds(pl.multiple_of(kt * tk, tk), tk), :]
        s_ref[rows, :] = lax.dot_general(q2[rows], k, (((1,), (1,)), ((), ())), preferred_element_type=F32)

    def update(kt, s_ref, r):
        rows = slice(r * rb, (r + 1) * rb)
        s = s_ref[rows, :]
        m_prev = m_ref[rows, :]
        m_new = jnp.maximum(m_prev, jnp.max(s, axis=1, keepdims=True))
        alpha = jnp.exp2(m_prev - m_new)
        p = jnp.exp2(s - m_new).astype(BF16)
        v_aug = jnp.concatenate([v_ref[0, pl.ds(pl.multiple_of(kt * tk, tk), tk), :], ones], axis=1)
        acc_ref[rows, :] = alpha * acc_ref[rows, :] + jnp.dot(p, v_aug, preferred_element_type=F32)
        m_ref[rows, :] = m_new

    def step(kt_next, s_next, kt, s_cur):
        lead = 0
        if kt_next is not None:
            for r in range(lead):
                scores(kt_next, s_next, r)
        for r in range(nrb):
            if kt_next is not None and r + lead < nrb:
                scores(kt_next, s_next, r + lead)
            update(kt, s_cur, r)

    m_ref[...] = jnp.full(m_ref.shape, -jnp.inf, F32)
    acc_ref[...] = jnp.zeros(acc_ref.shape, F32)
    for r in range(nrb):
        scores(0, s0_ref, r)

    def body(j, _):
        kt = 2 * j
        step(kt + 1, s1_ref, kt, s0_ref)
        step(kt + 2, s0_ref, kt + 1, s1_ref)
        return 0

    lax.fori_loop(0, nk // 2 - 1, body, 0)
    step(nk - 1, s1_ref, nk - 2, s0_ref)
    step(None, None, nk - 1, s1_ref)
    acc = acc_ref[...]
    o2 = acc[:, :HEAD_W] / acc[:, HEAD_W:]
    o = o2[:tq] - lam_ref[0] * o2[tq:]
    y = o * lax.rsqrt(jnp.mean(o * o, axis=-1, keepdims=True) + SUBLN_EPS) * g_ref[...] * post_scale
    o_ref[0] = (y * _silu(z_ref[0].astype(F32))).astype(o_ref.dtype)


def _diff_attention(lam, qk, v, z, gain, post_scale, tq=1024, tk=2048):
    b, s, w = v.shape
    nh = w // HEAD_W
    tq = min(tq, s)
    tk = min(tk, s // 2)
    assert s % (2 * tk) == 0 and s % tq == 0
    return pl.pallas_call(
        functools.partial(_attn_kernel, tk=tk, post_scale=post_scale),
        scratch_shapes=[
            pltpu.VMEM((2 * tq, tk), F32),
            pltpu.VMEM((2 * tq, tk), F32),
            pltpu.VMEM((2 * tq, 2 * HEAD_W), F32),
            pltpu.VMEM((2 * tq, 1), F32),
        ],
        grid=(b, nh, s // tq),
        in_specs=[
            pl.BlockSpec(memory_space=pltpu.SMEM),
            pl.BlockSpec((1, tq, HEAD_W), lambda bi, hi, qi: (bi, qi, hi)),
            pl.BlockSpec((1, s, HEAD_W), lambda bi, hi, qi: (bi, 0, nh + hi)),
            pl.BlockSpec((1, s, HEAD_W), lambda bi, hi, qi: (bi, 0, hi)),
            pl.BlockSpec((1, tq, HEAD_W), lambda bi, hi, qi: (bi, qi, hi)),
            pl.BlockSpec((1, HEAD_W), lambda bi, hi, qi: (0, 0)),
        ],
        out_specs=pl.BlockSpec((1, tq, HEAD_W), lambda bi, hi, qi: (bi, qi, hi)),
        out_shape=jax.ShapeDtypeStruct((b, s, w), BF16),
        compiler_params=_cparams(("parallel", "parallel", "arbitrary")),
        name="diff_attention",
    )(lam, qk, qk, v, z, gain.reshape(1, HEAD_W))


def _dot(a, b):
    return jnp.dot(a.astype(BF16), b.astype(BF16), preferred_element_type=F32)


def _dot_nt(a, b):
    return lax.dot_general(a.astype(BF16), b.astype(BF16), (((1,), (1,)), ((), ())), preferred_element_type=F32)


INV_BLOCK = 64
LOCAL_UNROLL = 4


def _unit_triangular_inverses(mats):
    n = mats[0].shape[0]
    row = lax.broadcasted_iota(jnp.int32, (n, n), 0)
    col = lax.broadcasted_iota(jnp.int32, (n, n), 1)
    same_block = (row // INV_BLOCK) == (col // INV_BLOCK)
    eye = jnp.where(row == col, 1.0, 0.0).astype(F32)
    a_diag = [jnp.where(same_block, a, 0.0) for a in mats]
    a_off = [jnp.where(same_block, 0.0, a) for a in mats]
    def mm(a16, b16):
        return jnp.dot(a16, b16, preferred_element_type=F32)

    ts = [eye - a for a in a_diag]
    a16 = [a.astype(BF16) for a in a_diag]
    qs = [mm(a, a) for a in a16]
    levels = int(math.log2(INV_BLOCK)) - 1
    for level in range(levels):
        q16 = [q.astype(BF16) for q in qs]
        t16 = [t.astype(BF16) for t in ts]
        if level < levels - 1:
            prods = [mm(q, jnp.concatenate([q, t], axis=1)) for q, t in zip(q16, t16)]
            qs = [p[:, :n] for p in prods]
            ts = [t + p[:, n:] for t, p in zip(ts, prods)]
        else:
            prods = [mm(q, t) for q, t in zip(q16, t16)]
            ts = [t + p for t, p in zip(ts, prods)]
    t16 = [t.astype(BF16) for t in ts]
    offs = [mm(a.astype(BF16), t) for a, t in zip(a_off, t16)]
    corr = [mm(t, o.astype(BF16)) for t, o in zip(t16, offs)]
    return [t - c for t, c in zip(ts, corr)]


def _conv_silu(x_ref, w_ref, blk, rows, n_blk):
    halo = 16
    t0 = pl.multiple_of(blk * rows, rows)
    xc = x_ref[0, pl.ds(t0, rows), :].astype(F32)
    prev_start = pl.multiple_of(jnp.maximum(t0 - halo, 0), halo)
    next_start = pl.multiple_of(jnp.minimum(t0 + rows, n_blk * rows - halo), halo)
    prev = jnp.where(blk == 0, 0.0, x_ref[0, pl.ds(prev_start, halo), :].astype(F32))
    nxt = jnp.where(blk == n_blk - 1, 0.0, x_ref[0, pl.ds(next_start, halo), :].astype(F32))
    xx = jnp.concatenate([prev, xc, nxt], axis=0)
    n = rows + 2 * halo
    x_m1 = pltpu.roll(xx, 1, 0)[halo:halo + rows]
    x_p1 = pltpu.roll(xx, n - 1, 0)[halo:halo + rows]
    x_p2 = pltpu.roll(xx, n - 2, 0)[halo:halo + rows]
    w = w_ref[...]
    y = x_m1 * w[0:1, :] + xc * w[1:2, :] + x_p1 * w[2:3, :] + x_p2 * w[3:4, :]
    return _silu(y)


def _gdn_kernel(q_ref, k_ref, v_ref, gcol_ref, grow_ref, z_ref, wq_ref, wk_ref, wv_ref, gain_ref, o_ref,
                qkv0_ref, qkv1_ref, u_ref, wqg_ref, sm_ref, sn_ref, attn_ref, egl_ref, dir_ref):
    s_len = q_ref.shape[1]
    n_chunks = s_len // CHUNK
    head = pl.program_id(1)
    row = lax.broadcasted_iota(jnp.int32, (CHUNK, CHUNK), 0)
    col = lax.broadcasted_iota(jnp.int32, (CHUNK, CHUNK), 1)
    lane = lax.broadcasted_iota(jnp.int32, (CHUNK, LANES), 1)

    def conv_group(j, qkv_ref):
        for g in range(LOCAL_UNROLL):
            n = j * LOCAL_UNROLL + g
            rows = pl.ds(g * CHUNK, CHUNK)
            q = _conv_silu(q_ref, wq_ref, n, CHUNK, n_chunks)
            k = _conv_silu(k_ref, wk_ref, n, CHUNK, n_chunks)
            qkv_ref[0, rows, :] = q * (lax.rsqrt(jnp.sum(q * q, axis=-1, keepdims=True) + NORM_EPS) * (HEAD_W ** -0.5))
            qkv_ref[1, rows, :] = k * lax.rsqrt(jnp.sum(k * k, axis=-1, keepdims=True) + NORM_EPS)
            qkv_ref[2, rows, :] = _conv_silu(v_ref, wv_ref, n, CHUNK, n_chunks)

    def local_group(j, qkv_ref):
        ns = [j * LOCAL_UNROLL + g for g in range(LOCAL_UNROLL)]
        t0s = [pl.multiple_of(n * CHUNK, CHUNK) for n in ns]
        qs, ks, vs = ([qkv_ref[i, pl.ds(g * CHUNK, CHUNK), :] for g in range(LOCAL_UNROLL)] for i in range(3))
        qk2s = [_dot_nt(jnp.concatenate([q, k], axis=0), k) for q, k in zip(qs, ks)]
        chains = []
        for g, n in enumerate(ns):
            gates = gcol_ref[0, pl.ds(t0s[g], CHUNK), :]
            for d in range(2):
                sel = head + d * N_HEADS
                gc = jnp.sum(jnp.where(lane == sel, gates, 0.0), axis=1, keepdims=True)
                beta = jnp.sum(jnp.where(lane == sel + 2 * N_HEADS, gates, 0.0), axis=1, keepdims=True)
                gc_row = grow_ref[sel, 0, pl.ds(n, 1), :]
                incl, strict, last = (row >= col, row > col, CHUNK - 1) if d == 0 else (row <= col, row < col, 0)
                decay = jnp.exp2(jnp.where(incl, gc - gc_row, MASKED))
                a = jnp.where(strict, qk2s[g][CHUNK:] * beta * decay, 0.0)
                chains.append((g, d, gc, beta, gc_row[:, last:last + 1], decay, a))
        ts = _unit_triangular_inverses([c[-1] for c in chains])
        egcs = [jnp.exp2(c[2]) for c in chains]
        uws = [_dot(t, jnp.concatenate([vs[c[0]] * c[3], ks[c[0]] * (c[3] * egc)], axis=1))
               for t, c, egc in zip(ts, chains, egcs)]
        kdts = [(ks[c[0]] * jnp.exp2(c[4] - c[2])).T for c in chains]
        trans = [_dot(kdt, uw) for kdt, uw in zip(kdts, uws)]
        for (g, d, gc, beta, gl, decay, _), egc, uw, tr in zip(chains, egcs, uws, trans):
            t0, n = t0s[g], ns[g]
            sn_ref[d, pl.ds(t0, CHUNK), :] = tr[:, :HEAD_W]
            sm_ref[d, pl.ds(t0, CHUNK), :] = tr[:, HEAD_W:].astype(BF16)
            u_ref[d, pl.ds(t0, CHUNK), :] = uw[:, :HEAD_W]
            wqg_ref[d, pl.ds(pl.multiple_of(2 * t0, CHUNK), CHUNK), :] = uw[:, HEAD_W:].astype(BF16)
            wqg_ref[d, pl.ds(pl.multiple_of(2 * t0 + CHUNK, CHUNK), CHUNK), :] = (qs[g] * egc).astype(BF16)
            attn_ref[d, pl.ds(t0, CHUNK), :] = (qk2s[g][:CHUNK] * decay).astype(BF16)
            egl_ref[d, pl.ds(n, 1), :] = jnp.broadcast_to(jnp.exp2(gl), (1, LANES))

    n_groups = n_chunks // LOCAL_UNROLL
    conv_group(0, qkv0_ref)

    def local_body(i, _):
        j = 2 * i
        conv_group(j + 1, qkv1_ref)
        local_group(j, qkv0_ref)
        conv_group(jnp.minimum(j + 2, n_groups - 1), qkv0_ref)
        local_group(j + 1, qkv1_ref)
        return 0

    lax.fori_loop(0, n_groups // 2, local_body, 0)

    def chunk_starts(i):
        return [pl.multiple_of(n * CHUNK, CHUNK) for n in (i, n_chunks - 1 - i)]

    def emit_outputs(i, v_new, rq):
        t0s = chunk_starts(i)
        intra = [jnp.dot(attn_ref[d, pl.ds(t0s[d], CHUNK), :], v_new[d], preferred_element_type=F32) for d in range(2)]
        for d in range(2):
            dir_ref[d, pl.ds(t0s[d], CHUNK), :] = rq[d] + intra[d]

    def state_step(i, carry):
        states, v_prev, rq_prev = carry
        t0s = chunk_starts(i)
        s16 = [states[d].astype(BF16) for d in range(2)]
        ms = [jnp.dot(sm_ref[d, pl.ds(t0s[d], CHUNK), :], s16[d], preferred_element_type=F32) for d in range(2)]
        rs = [jnp.dot(wqg_ref[d, pl.ds(pl.multiple_of(2 * t0s[d], 2 * CHUNK), 2 * CHUNK), :],
                      s16[d], preferred_element_type=F32) for d in range(2)]
        emit_outputs(jnp.maximum(i - 1, 0), v_prev, rq_prev)
        new_states = tuple(states[d] * egl_ref[d, pl.ds(n, 1), :] - ms[d] + sn_ref[d, pl.ds(t0s[d], CHUNK), :]
                           for d, n in enumerate((i, n_chunks - 1 - i)))
        v_new = tuple((u_ref[d, pl.ds(t0s[d], CHUNK), :] - rs[d][:CHUNK]).astype(BF16) for d in range(2))
        rq = tuple(rs[d][CHUNK:] for d in range(2))
        return new_states, v_new, rq

    zero = jnp.zeros((HEAD_W, HEAD_W), F32)
    zero_c = jnp.zeros((CHUNK, HEAD_W), F32)
    init = ((zero, zero), (zero_c.astype(BF16),) * 2, (zero_c, zero_c))
    _, v_last, rq_last = lax.fori_loop(0, n_chunks, state_step, init)
    emit_outputs(n_chunks - 1, v_last, rq_last)

    def finish_body(j, _):
        t0 = pl.multiple_of(j * (LOCAL_UNROLL * CHUNK), LOCAL_UNROLL * CHUNK)
        rows = pl.ds(t0, LOCAL_UNROLL * CHUNK)
        o = dir_ref[0, rows, :] + dir_ref[1, rows, :]
        y = o * lax.rsqrt(jnp.mean(o * o, axis=-1, keepdims=True) + NORM_EPS) * gain_ref[...]
        o_ref[0, rows, :] = (y * _silu(z_ref[0, rows, :].astype(F32))).astype(o_ref.dtype)
        return 0

    lax.fori_loop(0, n_groups, finish_body, 0)


def _gdn(qkv, gcol, grow, z, conv_w, gain):
    b, s, _ = qkv.shape
    nh = N_HEADS
    n_chunks = s // CHUNK
    assert s % CHUNK == 0 and n_chunks % (2 * LOCAL_UNROLL) == 0
    seq_spec = lambda off: pl.BlockSpec((1, s, HEAD_W), lambda bi, hi: (bi, 0, hi + off))
    w_spec = lambda off: pl.BlockSpec((CONV_W, HEAD_W), lambda bi, hi: (0, hi + off))
    return pl.pallas_call(
        _gdn_kernel,
        grid=(b, nh),
        in_specs=[
            seq_spec(0), seq_spec(nh), seq_spec(2 * nh),
            pl.BlockSpec((1, s, LANES), lambda bi, hi: (bi, 0, 0)),
            pl.BlockSpec((2 * nh, 1, n_chunks, CHUNK), lambda bi, hi: (0, bi, 0, 0)),
            seq_spec(nh),
            w_spec(0), w_spec(nh), w_spec(2 * nh),
            pl.BlockSpec((1, HEAD_W), lambda bi, hi: (0, 0)),
        ],
        out_specs=seq_spec(0),
        out_shape=jax.ShapeDtypeStruct((b, s, nh * HEAD_W), BF16),
        scratch_shapes=[
            pltpu.VMEM((3, LOCAL_UNROLL * CHUNK, HEAD_W), F32),
            pltpu.VMEM((3, LOCAL_UNROLL * CHUNK, HEAD_W), F32),
            pltpu.VMEM((2, s, HEAD_W), F32),
            pltpu.VMEM((2, 2 * s, HEAD_W), BF16),
            pltpu.VMEM((2, s, HEAD_W), BF16),
            pltpu.VMEM((2, s, HEAD_W), F32),
            pltpu.VMEM((2, s, CHUNK), BF16),
            pltpu.VMEM((2, n_chunks, LANES), F32),
            pltpu.VMEM((2, s, HEAD_W), F32),
        ],
        compiler_params=_cparams(("parallel", "arbitrary")),
        name="gated_delta",
    )(qkv, qkv, qkv, gcol, grow, z, conv_w, conv_w, conv_w, gain.reshape(1, HEAD_W))


def _merge_kernel(ya_ref, yb_ref, gate_ref, x_ref, wa_ref, wb_ref, wo_ref, g_ref, *out_refs, final, tn):
    d = x_ref.shape[1]
    merged = []
    for j in range(d // tn):
        sl = slice(j * tn, (j + 1) * tn)
        pa = jnp.dot(ya_ref[...], wa_ref[:, sl], preferred_element_type=F32)
        pb = jnp.dot(yb_ref[...], wb_ref[:, sl], preferred_element_type=F32)
        ga = gate_ref[:, j * tn:(j + 1) * tn].astype(F32)
        gb = gate_ref[:, d + j * tn:d + (j + 1) * tn].astype(F32)
        merged.append((ga * pa + gb * pb).astype(BF16))
    merged = jnp.concatenate(merged, axis=1)
    x = x_ref[...] + jnp.dot(merged, wo_ref[...], preferred_element_type=F32)
    y = x * lax.rsqrt(jnp.mean(x * x, axis=-1, keepdims=True) + NORM_EPS) * g_ref[...]
    if final:
        out_refs[0][...] = y
    else:
        out_refs[0][...] = x
        out_refs[1][...] = y.astype(BF16)


def _merge(ya, yb, gates, x, wa, wb, wo, g_next, final, tm=512, tn=256):
    m, d = x.shape
    tm = min(tm, m)
    row = lambda width: pl.BlockSpec((tm, width), lambda i: (i, 0))
    full = lambda r, c: pl.BlockSpec((r, c), lambda i: (0, 0))
    if final:
        out_shape = [jax.ShapeDtypeStruct((m, d), F32)]
        out_specs = [row(d)]
    else:
        out_shape = [jax.ShapeDtypeStruct((m, d), F32), jax.ShapeDtypeStruct((m, d), BF16)]
        out_specs = [row(d), row(d)]
    return pl.pallas_call(
        functools.partial(_merge_kernel, final=final, tn=tn),
        grid=(m // tm,),
        in_specs=[row(d), row(d), row(2 * d), row(d), full(d, d), full(d, d), full(d, d), full(1, d)],
        out_specs=out_specs,
        out_shape=out_shape,
        compiler_params=_cparams(("parallel",)),
        name="merge_out",
    )(ya, yb, gates, x, wa, wb, wo, g_next.reshape(1, d))


def _rope_tables(s):
    inv = ROPE_THETA ** (-jnp.arange(0, HD_QK, 2, dtype=F32) / HD_QK)
    ang = jnp.arange(s, dtype=F32)[:, None] * inv[None, :]
    ang = jnp.concatenate([ang, ang], -1)
    sign = jnp.where(jnp.arange(HD_QK) < HD_QK // 2, -1.0, 1.0).astype(F32)
    cos = jnp.tile(jnp.cos(ang), (1, LANES // HD_QK))
    sin = jnp.tile(jnp.sin(ang) * sign[None, :], (1, LANES // HD_QK))
    return cos, sin


def _trunk(x, norm_g, w_in, conv_w, lam_qk, diff_norm_g, a_log, dt_bias, gdn_norm_g, w_branch, w_out, final_g):
    b, s, d = x.shape
    depth = w_in.shape[0]
    m = b * s
    w_a = N_HEADS * HEAD_W
    qk_w = N_HEADS * 2 * HD_QK
    c_q, c_k, c_v, c_za = 0, qk_w, 2 * qk_w, 2 * qk_w + w_a
    c_qkvb = c_za + w_a
    c_zb = c_qkvb + 3 * w_a
    c_a = c_zb + w_a
    c_b = c_a + 2 * N_HEADS
    c_gate = c_b + 2 * N_HEADS

    cos, sin = _rope_tables(s)
    tm = min(1024, s)
    pos_blocks = s // tm
    rope_specs = [pl.BlockSpec((tm, LANES), lambda i: (i % pos_blocks, 0))] * 2

    x2 = x.reshape(m, d)
    h = _rmsnorm(x2, norm_g[0], BF16)
    for l in range(depth):
        w = w_in[l]
        scale = HD_QK ** -0.5 * math.log2(math.e)
        w_qk = jnp.concatenate([w[:, c_q:c_k] * scale, w[:, c_k:c_v]], axis=1).astype(BF16)
        w_v = w[:, c_v:c_za].astype(BF16)
        w_z = jnp.concatenate([w[:, c_za:c_qkvb], w[:, c_zb:c_a]], axis=1).astype(BF16)
        w_qkvb = w[:, c_qkvb:c_zb].astype(BF16)
        w_ab = jnp.pad(w[:, c_a:c_gate], ((0, 0), (0, LANES - 4 * N_HEADS))).astype(BF16)
        w_gate = w[:, c_gate:].astype(BF16)
        alog_vec = jnp.pad(a_log[l].reshape(1, -1), ((0, 0), (0, LANES - 2 * N_HEADS)))
        dtb_vec = jnp.pad(dt_bias[l].reshape(1, -1), ((0, 0), (0, LANES - 2 * N_HEADS)))

        qk = _proj(_proj_rope_kernel, h, w_qk, (cos, sin), rope_specs, BF16, tm, "proj_qk_rope")
        v_a = _proj(_proj_plain_kernel, h, w_v, (), [], BF16, tm, "proj_v")
        z_ab = _proj(_proj_plain_kernel, h, w_z, (), [], BF16, tm, "proj_z")
        qkv_b = _proj(_proj_plain_kernel, h, w_qkvb, (), [], BF16, tm, "proj_qkv_delta")
        gcol, grow = _proj_gdn_gates(h, w_ab, alog_vec, dtb_vec, tm)
        merge_gates = _proj(_proj_sigmoid_kernel, h, w_gate, (), [], BF16, tm, "proj_merge_gates")

        lam_init = 0.8 - 0.6 * math.exp(-0.3 * l)
        lq = lam_qk[l].astype(F32)
        lam = (jnp.exp(jnp.sum(lq[0] * lq[1])) - jnp.exp(jnp.sum(lq[2] * lq[3])) + lam_init).reshape(1)

        z_ab = z_ab.reshape(b, s, 2 * w_a)
        y_a = _diff_attention(lam, qk.reshape(b, s, 2 * qk_w), v_a.reshape(b, s, w_a), z_ab,
                              diff_norm_g[l], 1.0 - lam_init)
        y_b = _gdn(qkv_b.reshape(b, s, 3 * w_a), gcol.reshape(b, s, LANES),
                   grow.reshape(2 * N_HEADS, b, s // CHUNK, CHUNK), z_ab, conv_w[l], gdn_norm_g[l])

        final = l == depth - 1
        g_next = final_g if final else norm_g[l + 1]
        outs = _merge(y_a.reshape(m, w_a), y_b.reshape(m, w_a), merge_gates, x2,
                      w_branch[l, 0].astype(BF16), w_branch[l, 1].astype(BF16), w_out[l].astype(BF16),
                      g_next, final)
        if final:
            x2 = outs[0]
        else:
            x2, h = outs
    return x2.reshape(b, s, d)


def kernel(x_prompt, x_sample, norm_g, w_in, conv_w, lam_qk, diff_norm_g, a_log, dt_bias, gdn_norm_g, w_branch, w_out, final_g):
    params = (norm_g, w_in, conv_w, lam_qk, diff_norm_g, a_log, dt_bias, gdn_norm_g, w_branch, w_out, final_g)
    return (_trunk(x_prompt, *params), _trunk(x_sample, *params))
```

```python
import functools
import math

import jax
import jax.numpy as jnp
from jax import lax
from jax.experimental import pallas as pl
from jax.experimental.pallas import tpu as pltpu

F32 = jnp.float32
BF16 = jnp.bfloat16

LANES = 128
N_HEADS = 8
HD_QK = 64
HEAD_W = 128
CONV_W = 4
CHUNK = 128
ROPE_THETA = 10000.0
NORM_EPS = 1e-6
SUBLN_EPS = 1e-5
VMEM_LIMIT = 56 * 1024 * 1024
MASKED = -1e30


def _cparams(sem):
    return pltpu.CompilerParams(dimension_semantics=sem, vmem_limit_bytes=VMEM_LIMIT)


def _sigmoid(x):
    return 0.5 * jnp.tanh(0.5 * x) + 0.5


def _silu(x):
    h = 0.5 * x
    return h * jnp.tanh(h) + h


def _rmsnorm_kernel(x_ref, g_ref, o_ref, *, eps):
    x = x_ref[...]
    y = x * lax.rsqrt(jnp.mean(x * x, axis=-1, keepdims=True) + eps)
    o_ref[...] = (y * g_ref[...]).astype(o_ref.dtype)


def _rmsnorm(x, g, out_dtype, tm=1024):
    m, d = x.shape
    tm = min(tm, m)
    return pl.pallas_call(
        functools.partial(_rmsnorm_kernel, eps=NORM_EPS),
        grid=(m // tm,),
        in_specs=[pl.BlockSpec((tm, d), lambda i: (i, 0)), pl.BlockSpec((1, d), lambda i: (0, 0))],
        out_specs=pl.BlockSpec((tm, d), lambda i: (i, 0)),
        out_shape=jax.ShapeDtypeStruct((m, d), out_dtype),
        compiler_params=_cparams(("parallel",)),
        name="rmsnorm",
    )(x, g.reshape(1, d))


PROJ_COLS = 512


def _proj_plain_kernel(h_ref, w_ref, o_ref):
    h = h_ref[...]
    for j in range(o_ref.shape[1] // PROJ_COLS):
        cols = slice(j * PROJ_COLS, (j + 1) * PROJ_COLS)
        o_ref[:, cols] = jnp.dot(h, w_ref[:, cols], preferred_element_type=F32).astype(o_ref.dtype)


def _proj_sigmoid_kernel(h_ref, w_ref, o_ref):
    h = h_ref[...]
    for j in range(o_ref.shape[1] // PROJ_COLS):
        cols = slice(j * PROJ_COLS, (j + 1) * PROJ_COLS)
        acc = jnp.dot(h, w_ref[:, cols], preferred_element_type=F32)
        o_ref[:, cols] = _sigmoid(acc).astype(o_ref.dtype)


def _proj_rope_kernel(h_ref, w_ref, cos_ref, sin_ref, o_ref):
    h = h_ref[...]
    cos = cos_ref[...]
    sin = sin_ref[...]
    lane = lax.broadcasted_iota(jnp.int32, cos.shape, 1)
    first_half = (lane % HD_QK) < (HD_QK // 2)
    for j in range(o_ref.shape[1] // PROJ_COLS):
        acc = jnp.dot(h, w_ref[:, j * PROJ_COLS:(j + 1) * PROJ_COLS], preferred_element_type=F32)
        for g in range(PROJ_COLS // LANES):
            a = acc[:, g * LANES:(g + 1) * LANES]
            rot = jnp.where(first_half, pltpu.roll(a, LANES - HD_QK // 2, 1), pltpu.roll(a, HD_QK // 2, 1))
            lanes = slice(j * PROJ_COLS + g * LANES, j * PROJ_COLS + (g + 1) * LANES)
            o_ref[:, lanes] = (a * cos + rot * sin).astype(o_ref.dtype)


def _proj(kernel_fn, h, w, extras, extra_specs, out_dtype, tm, name):
    m, d = h.shape
    n = w.shape[1]
    return pl.pallas_call(
        kernel_fn,
        grid=(m // tm,),
        in_specs=[pl.BlockSpec((tm, d), lambda i: (i, 0)), pl.BlockSpec((d, n), lambda i: (0, 0))] + extra_specs,
        out_specs=pl.BlockSpec((tm, n), lambda i: (i, 0)),
        out_shape=jax.ShapeDtypeStruct((m, n), out_dtype),
        compiler_params=_cparams(("parallel",)),
        name=name,
    )(h, w, *extras)


def _proj_gdn_gate_kernel(h_ref, w_ref, alog_ref, dtb_ref, col_ref, row_ref):
    acc = jnp.dot(h_ref[...], w_ref[...], preferred_element_type=F32)
    lane = lax.broadcasted_iota(jnp.int32, acc.shape, 1)
    x = acc + dtb_ref[...]
    softplus = jnp.maximum(x, 0.0) + jnp.log(1.0 + jnp.exp(-jnp.abs(x)))
    g = jnp.where(lane < 2 * N_HEADS, (-math.log2(math.e)) * jnp.exp(alog_ref[...]) * softplus, 0.0)
    beta = _sigmoid(acc)
    r = lax.broadcasted_iota(jnp.int32, (CHUNK, CHUNK), 0)
    c = lax.broadcasted_iota(jnp.int32, (CHUNK, CHUNK), 1)
    tril = jnp.where(r >= c, 1.0, 0.0).astype(F32)
    lane_c = lax.broadcasted_iota(jnp.int32, (CHUNK, LANES), 1)
    for ci in range(acc.shape[0] // CHUNK):
        rows = slice(ci * CHUNK, (ci + 1) * CHUNK)
        g_c = g[rows]
        prefix = jnp.dot(tril, g_c, preferred_element_type=F32, precision=lax.Precision.HIGHEST)
        suffix = prefix[CHUNK - 1:CHUNK, :] - prefix + g_c
        out = jnp.where(lane_c < N_HEADS, prefix, jnp.where(lane_c < 2 * N_HEADS, suffix, beta[rows]))
        col_ref[rows, :] = out
        row_ref[:, rows] = out.T[:2 * N_HEADS, :]


def _proj_gdn_gates(h, w, alog_vec, dtb_vec, tm):
    m, d = h.shape
    vec = pl.BlockSpec((1, LANES), lambda i: (0, 0))
    return pl.pallas_call(
        _proj_gdn_gate_kernel,
        grid=(m // tm,),
        in_specs=[pl.BlockSpec((tm, d), lambda i: (i, 0)), pl.BlockSpec((d, LANES), lambda i: (0, 0)), vec, vec],
        out_specs=[pl.BlockSpec((tm, LANES), lambda i: (i, 0)), pl.BlockSpec((2 * N_HEADS, tm), lambda i: (0, i))],
        out_shape=[jax.ShapeDtypeStruct((m, LANES), F32), jax.ShapeDtypeStruct((2 * N_HEADS, m), F32)],
        compiler_params=_cparams(("parallel",)),
        name="proj_delta_gates",
    )(h, w, alog_vec, dtb_vec)


ATTN_ROW_BLOCKS = 4


def _attn_kernel(lam_ref, q_ref, k_ref, v_ref, z_ref, g_ref, o_ref, s0_ref, s1_ref, acc_ref, m_ref, *, tk, post_scale):
    tq = q_ref.shape[1]
    s_len = k_ref.shape[1]
    nk = s_len // tk
    nrb = ATTN_ROW_BLOCKS
    rb = 2 * tq // nrb
    q = q_ref[0]
    lane = lax.broadcasted_iota(jnp.int32, q.shape, 1)
    zero = jnp.zeros_like(q)
    q2 = jnp.concatenate([jnp.where(lane < HD_QK, q, zero), jnp.where(lane >= HD_QK, q, zero)], axis=0)
    ones = jnp.ones((tk, HEAD_W), BF16)

    def scores(kt, s_ref, r):
        rows = slice(r * rb, (r + 1) * rb)
        k = k_ref[0, pl.ds(pl.multiple_of(kt * tk, tk), tk), :]
        s_ref[rows, :] = lax.dot_general(q2[rows], k, (((1,), (1,)), ((), ())), preferred_element_type=F32)

    def update(kt, s_ref, r):
        rows = slice(r * rb, (r + 1) * rb)
        s = s_ref[rows, :]
        m_prev = m_ref[rows, :]
        m_new = jnp.maximum(m_prev, jnp.max(s, axis=1, keepdims=True))
        alpha = jnp.exp2(m_prev - m_new)
        p = jnp.exp2(s - m_new).astype(BF16)
        v_aug = jnp.concatenate([v_ref[0, pl.ds(pl.multiple_of(kt * tk, tk), tk), :], ones], axis=1)
        acc_ref[rows, :] = alpha * acc_ref[rows, :] + jnp.dot(p, v_aug, preferred_element_type=F32)
        m_ref[rows, :] = m_new

    def step(kt_next, s_next, kt, s_cur):
        for r in range(nrb):
            if kt_next is not None:
                scores(kt_next, s_next, r)
            update(kt, s_cur, r)

    m_ref[...] = jnp.full(m_ref.shape, -jnp.inf, F32)
    acc_ref[...] = jnp.zeros(acc_ref.shape, F32)
    for r in range(nrb):
        scores(0, s0_ref, r)

    def body(j, _):
        kt = 2 * j
        step(kt + 1, s1_ref, kt, s0_ref)
        step(kt + 2, s0_ref, kt + 1, s1_ref)
        return 0

    lax.fori_loop(0, nk // 2 - 1, body, 0)
    step(nk - 1, s1_ref, nk - 2, s0_ref)
    step(None, None, nk - 1, s1_ref)
    acc = acc_ref[...]
    o2 = acc[:, :HEAD_W] / acc[:, HEAD_W:]
    o = o2[:tq] - lam_ref[0] * o2[tq:]
    y = o * lax.rsqrt(jnp.mean(o * o, axis=-1, keepdims=True) + SUBLN_EPS) * g_ref[...] * post_scale
    o_ref[0] = (y * _silu(z_ref[0].astype(F32))).astype(o_ref.dtype)


def _diff_attention(lam, qk, v, z, gain, post_scale, tq=1024, tk=2048):
    b, s, w = v.shape
    nh = w // HEAD_W
    tq = min(tq, s)
    tk = min(tk, s // 2)
    assert s % (2 * tk) == 0 and s % tq == 0
    return pl.pallas_call(
        functools.partial(_attn_kernel, tk=tk, post_scale=post_scale),
        scratch_shapes=[
            pltpu.VMEM((2 * tq, tk), F32),
            pltpu.VMEM((2 * tq, tk), F32),
            pltpu.VMEM((2 * tq, 2 * HEAD_W), F32),
            pltpu.VMEM((2 * tq, 1), F32),
        ],
        grid=(b, nh, s // tq),
        in_specs=[
            pl.BlockSpec(memory_space=pltpu.SMEM),
            pl.BlockSpec((1, tq, HEAD_W), lambda bi, hi, qi: (bi, qi, hi)),
            pl.BlockSpec((1, s, HEAD_W), lambda bi, hi, qi: (bi, 0, nh + hi)),
            pl.BlockSpec((1, s, HEAD_W), lambda bi, hi, qi: (bi, 0, hi)),
            pl.BlockSpec((1, tq, HEAD_W), lambda bi, hi, qi: (bi, qi, hi)),
            pl.BlockSpec((1, HEAD_W), lambda bi, hi, qi: (0, 0)),
        ],
        out_specs=pl.BlockSpec((1, tq, HEAD_W), lambda bi, hi, qi: (bi, qi, hi)),
        out_shape=jax.ShapeDtypeStruct((b, s, w), BF16),
        compiler_params=_cparams(("parallel", "parallel", "arbitrary")),
        name="diff_attention",
    )(lam, qk, qk, v, z, gain.reshape(1, HEAD_W))


def _dot(a, b):
    return jnp.dot(a.astype(BF16), b.astype(BF16), preferred_element_type=F32)


def _dot_nt(a, b):
    return lax.dot_general(a.astype(BF16), b.astype(BF16), (((1,), (1,)), ((), ())), preferred_element_type=F32)


INV_BLOCK = 64
LOCAL_UNROLL = 4
STATE_UNROLL = 8


def _unit_triangular_inverses(mats):
    n = mats[0].shape[0]
    row = lax.broadcasted_iota(jnp.int32, (n, n), 0)
    col = lax.broadcasted_iota(jnp.int32, (n, n), 1)
    same_block = (row // INV_BLOCK) == (col // INV_BLOCK)
    eye = jnp.where(row == col, 1.0, 0.0).astype(F32)
    a_diag = [jnp.where(same_block, a, 0.0) for a in mats]
    a_off = [jnp.where(same_block, 0.0, a) for a in mats]
    def mm(a16, b16):
        return jnp.dot(a16, b16, preferred_element_type=F32)

    ts = [eye - a for a in a_diag]
    a16 = [a.astype(BF16) for a in a_diag]
    qs = [mm(a, a) for a in a16]
    levels = int(math.log2(INV_BLOCK)) - 1
    for level in range(levels):
        q16 = [q.astype(BF16) for q in qs]
        t16 = [t.astype(BF16) for t in ts]
        if level < levels - 1:
            prods = [mm(q, jnp.concatenate([q, t], axis=1)) for q, t in zip(q16, t16)]
            qs = [p[:, :n] for p in prods]
            ts = [t + p[:, n:] for t, p in zip(ts, prods)]
        else:
            prods = [mm(q, t) for q, t in zip(q16, t16)]
            ts = [t + p for t, p in zip(ts, prods)]
    t16 = [t.astype(BF16) for t in ts]
    offs = [mm(a.astype(BF16), t) for a, t in zip(a_off, t16)]
    corr = [mm(t, o.astype(BF16)) for t, o in zip(t16, offs)]
    return [t - c for t, c in zip(ts, corr)]


def _conv_silu(x_ref, w_ref, blk, rows, n_blk):
    halo = 16
    t0 = pl.multiple_of(blk * rows, rows)
    xc = x_ref[0, pl.ds(t0, rows), :].astype(F32)
    prev_start = pl.multiple_of(jnp.maximum(t0 - halo, 0), halo)
    next_start = pl.multiple_of(jnp.minimum(t0 + rows, n_blk * rows - halo), halo)
    prev = jnp.where(blk == 0, 0.0, x_ref[0, pl.ds(prev_start, halo), :].astype(F32))
    nxt = jnp.where(blk == n_blk - 1, 0.0, x_ref[0, pl.ds(next_start, halo), :].astype(F32))
    xx = jnp.concatenate([prev, xc, nxt], axis=0)
    n = rows + 2 * halo
    x_m1 = pltpu.roll(xx, 1, 0)[halo:halo + rows]
    x_p1 = pltpu.roll(xx, n - 1, 0)[halo:halo + rows]
    x_p2 = pltpu.roll(xx, n - 2, 0)[halo:halo + rows]
    w = w_ref[...]
    y = x_m1 * w[0:1, :] + xc * w[1:2, :] + x_p1 * w[2:3, :] + x_p2 * w[3:4, :]
    return _silu(y)


def _gdn_kernel(q_ref, k_ref, v_ref, gcol_ref, grow_ref, z_ref, wq_ref, wk_ref, wv_ref, gain_ref, o_ref,
                qkv0_ref, qkv1_ref, u_ref, wqg_ref, sm_ref, sn_ref, attn_ref, egl_ref, dir_ref):
    s_len = q_ref.shape[1]
    n_chunks = s_len // CHUNK
    head = pl.program_id(1)
    row = lax.broadcasted_iota(jnp.int32, (CHUNK, CHUNK), 0)
    col = lax.broadcasted_iota(jnp.int32, (CHUNK, CHUNK), 1)
    lane = lax.broadcasted_iota(jnp.int32, (CHUNK, LANES), 1)

    def conv_group(j, qkv_ref):
        for g in range(LOCAL_UNROLL):
            n = j * LOCAL_UNROLL + g
            rows = pl.ds(g * CHUNK, CHUNK)
            q = _conv_silu(q_ref, wq_ref, n, CHUNK, n_chunks)
            k = _conv_silu(k_ref, wk_ref, n, CHUNK, n_chunks)
            qkv_ref[0, rows, :] = q * (lax.rsqrt(jnp.sum(q * q, axis=-1, keepdims=True) + NORM_EPS) * (HEAD_W ** -0.5))
            qkv_ref[1, rows, :] = k * lax.rsqrt(jnp.sum(k * k, axis=-1, keepdims=True) + NORM_EPS)
            qkv_ref[2, rows, :] = _conv_silu(v_ref, wv_ref, n, CHUNK, n_chunks)

    def local_group(j, qkv_ref):
        ns = [j * LOCAL_UNROLL + g for g in range(LOCAL_UNROLL)]
        t0s = [pl.multiple_of(n * CHUNK, CHUNK) for n in ns]
        qs, ks, vs = ([qkv_ref[i, pl.ds(g * CHUNK, CHUNK), :] for g in range(LOCAL_UNROLL)] for i in range(3))
        qk2s = [_dot_nt(jnp.concatenate([q, k], axis=0), k) for q, k in zip(qs, ks)]
        chains = []
        for g, n in enumerate(ns):
            gates = gcol_ref[0, pl.ds(t0s[g], CHUNK), :]
            for d in range(2):
                sel = head + d * N_HEADS
                gc = jnp.sum(jnp.where(lane == sel, gates, 0.0), axis=1, keepdims=True)
                beta = jnp.sum(jnp.where(lane == sel + 2 * N_HEADS, gates, 0.0), axis=1, keepdims=True)
                gc_row = grow_ref[sel, 0, pl.ds(n, 1), :]
                incl, strict, last = (row >= col, row > col, CHUNK - 1) if d == 0 else (row <= col, row < col, 0)
                decay = jnp.exp2(jnp.where(incl, gc - gc_row, MASKED))
                a = jnp.where(strict, qk2s[g][CHUNK:] * beta * decay, 0.0)
                chains.append((g, d, gc, beta, gc_row[:, last:last + 1], decay, a))
        ts = _unit_triangular_inverses([c[-1] for c in chains])
        egcs = [jnp.exp2(c[2]) for c in chains]
        uws = [_dot(t, jnp.concatenate([vs[c[0]] * c[3], ks[c[0]] * (c[3] * egc)], axis=1))
               for t, c, egc in zip(ts, chains, egcs)]
        kdts = [(ks[c[0]] * jnp.exp2(c[4] - c[2])).T for c in chains]
        trans = [_dot(kdt, uw) for kdt, uw in zip(kdts, uws)]
        for (g, d, gc, beta, gl, decay, _), egc, uw, tr in zip(chains, egcs, uws, trans):
            t0, n = t0s[g], ns[g]
            sn_ref[d, pl.ds(t0, CHUNK), :] = tr[:, :HEAD_W]
            sm_ref[d, pl.ds(t0, CHUNK), :] = tr[:, HEAD_W:].astype(BF16)
            u_ref[d, pl.ds(t0, CHUNK), :] = uw[:, :HEAD_W]
            wqg_ref[d, pl.ds(pl.multiple_of(2 * t0, CHUNK), CHUNK), :] = uw[:, HEAD_W:].astype(BF16)
            wqg_ref[d, pl.ds(pl.multiple_of(2 * t0 + CHUNK, CHUNK), CHUNK), :] = (qs[g] * egc).astype(BF16)
            attn_ref[d, pl.ds(t0, CHUNK), :] = (qk2s[g][:CHUNK] * decay).astype(BF16)
            egl_ref[d, pl.ds(n, 1), :] = jnp.broadcast_to(jnp.exp2(gl), (1, LANES))

    n_groups = n_chunks // LOCAL_UNROLL
    conv_group(0, qkv0_ref)

    def local_body(i, _):
        j = 2 * i
        conv_group(j + 1, qkv1_ref)
        local_group(j, qkv0_ref)
        conv_group(jnp.minimum(j + 2, n_groups - 1), qkv0_ref)
        local_group(j + 1, qkv1_ref)
        return 0

    lax.fori_loop(0, n_groups // 2, local_body, 0)

    def chunk_starts(i):
        return [pl.multiple_of(n * CHUNK, CHUNK) for n in (i, n_chunks - 1 - i)]

    def emit_outputs(i, v_new, rq):
        t0s = chunk_starts(i)
        intra = [jnp.dot(attn_ref[d, pl.ds(t0s[d], CHUNK), :], v_new[d], preferred_element_type=F32) for d in range(2)]
        for d in range(2):
            dir_ref[d, pl.ds(t0s[d], CHUNK), :] = rq[d] + intra[d]

    def state_step(i, states, pending):
        t0s = chunk_starts(i)
        s16 = [states[d].astype(BF16) for d in range(2)]
        ms = [jnp.dot(sm_ref[d, pl.ds(t0s[d], CHUNK), :], s16[d], preferred_element_type=F32) for d in range(2)]
        rs = [jnp.dot(wqg_ref[d, pl.ds(pl.multiple_of(2 * t0s[d], 2 * CHUNK), 2 * CHUNK), :],
                      s16[d], preferred_element_type=F32) for d in range(2)]
        if pending is not None:
            emit_outputs(*pending)
        new_states = tuple(states[d] * egl_ref[d, pl.ds(n, 1), :] - ms[d] + sn_ref[d, pl.ds(t0s[d], CHUNK), :]
                           for d, n in enumerate((i, n_chunks - 1 - i)))
        v_new = tuple((u_ref[d, pl.ds(t0s[d], CHUNK), :] - rs[d][:CHUNK]).astype(BF16) for d in range(2))
        rq = tuple(rs[d][CHUNK:] for d in range(2))
        return new_states, (i, v_new, rq)

    def state_steps(j, states):
        pending = None
        for g in range(STATE_UNROLL):
            states, pending = state_step(STATE_UNROLL * j + g, states, pending)
        emit_outputs(*pending)
        return states

    zero = jnp.zeros((HEAD_W, HEAD_W), F32)
    lax.fori_loop(0, n_chunks // STATE_UNROLL, state_steps, (zero, zero))

    def finish_body(j, _):
        t0 = pl.multiple_of(j * (LOCAL_UNROLL * CHUNK), LOCAL_UNROLL * CHUNK)
        rows = pl.ds(t0, LOCAL_UNROLL * CHUNK)
        o = dir_ref[0, rows, :] + dir_ref[1, rows, :]
        y = o * lax.rsqrt(jnp.mean(o * o, axis=-1, keepdims=True) + NORM_EPS) * gain_ref[...]
        o_ref[0, rows, :] = (y * _silu(z_ref[0, rows, :].astype(F32))).astype(o_ref.dtype)
        return 0

    lax.fori_loop(0, n_groups, finish_body, 0)


def _gdn(qkv, gcol, grow, z, conv_w, gain):
    b, s, _ = qkv.shape
    nh = N_HEADS
    n_chunks = s // CHUNK
    assert s % CHUNK == 0 and n_chunks % (2 * LOCAL_UNROLL) == 0 and n_chunks % STATE_UNROLL == 0
    seq_spec = lambda off: pl.BlockSpec((1, s, HEAD_W), lambda bi, hi: (bi, 0, hi + off))
    w_spec = lambda off: pl.BlockSpec((CONV_W, HEAD_W), lambda bi, hi: (0, hi + off))
    return pl.pallas_call(
        _gdn_kernel,
        grid=(b, nh),
        in_specs=[
            seq_spec(0), seq_spec(nh), seq_spec(2 * nh),
            pl.BlockSpec((1, s, LANES), lambda bi, hi: (bi, 0, 0)),
            pl.BlockSpec((2 * nh, 1, n_chunks, CHUNK), lambda bi, hi: (0, bi, 0, 0)),
            seq_spec(nh),
            w_spec(0), w_spec(nh), w_spec(2 * nh),
            pl.BlockSpec((1, HEAD_W), lambda bi, hi: (0, 0)),
        ],
        out_specs=seq_spec(0),
        out_shape=jax.ShapeDtypeStruct((b, s, nh * HEAD_W), BF16),
        scratch_shapes=[
            pltpu.VMEM((3, LOCAL_UNROLL * CHUNK, HEAD_W), F32),
            pltpu.VMEM((3, LOCAL_UNROLL * CHUNK, HEAD_W), F32),
            pltpu.VMEM((2, s, HEAD_W), F32),
            pltpu.VMEM((2, 2 * s, HEAD_W), BF16),
            pltpu.VMEM((2, s, HEAD_W), BF16),
            pltpu.VMEM((2, s, HEAD_W), F32),
            pltpu.VMEM((2, s, CHUNK), BF16),
            pltpu.VMEM((2, n_chunks, LANES), F32),
            pltpu.VMEM((2, s, HEAD_W), F32),
        ],
        compiler_params=_cparams(("parallel", "arbitrary")),
        name="gated_delta",
    )(qkv, qkv, qkv, gcol, grow, z, conv_w, conv_w, conv_w, gain.reshape(1, HEAD_W))


def _merge_kernel(ya_ref, yb_ref, gate_ref, x_ref, wa_ref, wb_ref, wo_ref, g_ref, *out_refs, final, tn):
    d = x_ref.shape[1]
    merged = []
    for j in range(d // tn):
        sl = slice(j * tn, (j + 1) * tn)
        pa = jnp.dot(ya_ref[...], wa_ref[:, sl], preferred_element_type=F32)
        pb = jnp.dot(yb_ref[...], wb_ref[:, sl], preferred_element_type=F32)
        ga = gate_ref[:, j * tn:(j + 1) * tn].astype(F32)
        gb = gate_ref[:, d + j * tn:d + (j + 1) * tn].astype(F32)
        merged.append((ga * pa + gb * pb).astype(BF16))
    merged = jnp.concatenate(merged, axis=1)
    x = x_ref[...] + jnp.dot(merged, wo_ref[...], preferred_element_type=F32)
    y = x * lax.rsqrt(jnp.mean(x * x, axis=-1, keepdims=True) + NORM_EPS) * g_ref[...]
    if final:
        out_refs[0][...] = y
    else:
        out_refs[0][...] = x
        out_refs[1][...] = y.astype(BF16)


def _merge(ya, yb, gates, x, wa, wb, wo, g_next, final, tm=512, tn=256):
    m, d = x.shape
    tm = min(tm, m)
    row = lambda width: pl.BlockSpec((tm, width), lambda i: (i, 0))
    full = lambda r, c: pl.BlockSpec((r, c), lambda i: (0, 0))
    if final:
        out_shape = [jax.ShapeDtypeStruct((m, d), F32)]
        out_specs = [row(d)]
    else:
        out_shape = [jax.ShapeDtypeStruct((m, d), F32), jax.ShapeDtypeStruct((m, d), BF16)]
        out_specs = [row(d), row(d)]
    return pl.pallas_call(
        functools.partial(_merge_kernel, final=final, tn=tn),
        grid=(m // tm,),
        in_specs=[row(d), row(d), row(2 * d), row(d), full(d, d), full(d, d), full(d, d), full(1, d)],
        out_specs=out_specs,
        out_shape=out_shape,
        compiler_params=_cparams(("parallel",)),
        name="merge_out",
    )(ya, yb, gates, x, wa, wb, wo, g_next.reshape(1, d))


def _rope_tables(s):
    inv = ROPE_THETA ** (-jnp.arange(0, HD_QK, 2, dtype=F32) / HD_QK)
    ang = jnp.arange(s, dtype=F32)[:, None] * inv[None, :]
    ang = jnp.concatenate([ang, ang], -1)
    sign = jnp.where(jnp.arange(HD_QK) < HD_QK // 2, -1.0, 1.0).astype(F32)
    cos = jnp.tile(jnp.cos(ang), (1, LANES // HD_QK))
    sin = jnp.tile(jnp.sin(ang) * sign[None, :], (1, LANES // HD_QK))
    return cos, sin


def _trunk(x, norm_g, w_in, conv_w, lam_qk, diff_norm_g, a_log, dt_bias, gdn_norm_g, w_branch, w_out, final_g):
    b, s, d = x.shape
    depth = w_in.shape[0]
    m = b * s
    w_a = N_HEADS * HEAD_W
    qk_w = N_HEADS * 2 * HD_QK
    c_q, c_k, c_v, c_za = 0, qk_w, 2 * qk_w, 2 * qk_w + w_a
    c_qkvb = c_za + w_a
    c_zb = c_qkvb + 3 * w_a
    c_a = c_zb + w_a
    c_b = c_a + 2 * N_HEADS
    c_gate = c_b + 2 * N_HEADS

    cos, sin = _rope_tables(s)
    tm = min(1024, s)
    pos_blocks = s // tm
    rope_specs = [pl.BlockSpec((tm, LANES), lambda i: (i % pos_blocks, 0))] * 2

    x2 = x.reshape(m, d)
    h = _rmsnorm(x2, norm_g[0], BF16)
    for l in range(depth):
        w = w_in[l]
        scale = HD_QK ** -0.5 * math.log2(math.e)
        w_qk = jnp.concatenate([w[:, c_q:c_k] * scale, w[:, c_k:c_v]], axis=1).astype(BF16)
        w_v = w[:, c_v:c_za].astype(BF16)
        w_z = jnp.concatenate([w[:, c_za:c_qkvb], w[:, c_zb:c_a]], axis=1).astype(BF16)
        w_qkvb = w[:, c_qkvb:c_zb].astype(BF16)
        w_ab = jnp.pad(w[:, c_a:c_gate], ((0, 0), (0, LANES - 4 * N_HEADS))).astype(BF16)
        w_gate = w[:, c_gate:].astype(BF16)
        alog_vec = jnp.pad(a_log[l].reshape(1, -1), ((0, 0), (0, LANES - 2 * N_HEADS)))
        dtb_vec = jnp.pad(dt_bias[l].reshape(1, -1), ((0, 0), (0, LANES - 2 * N_HEADS)))

        qk = _proj(_proj_rope_kernel, h, w_qk, (cos, sin), rope_specs, BF16, tm, "proj_qk_rope")
        v_a = _proj(_proj_plain_kernel, h, w_v, (), [], BF16, tm, "proj_v")
        z_ab = _proj(_proj_plain_kernel, h, w_z, (), [], BF16, tm, "proj_z")
        qkv_b = _proj(_proj_plain_kernel, h, w_qkvb, (), [], BF16, tm, "proj_qkv_delta")
        gcol, grow = _proj_gdn_gates(h, w_ab, alog_vec, dtb_vec, tm)
        merge_gates = _proj(_proj_sigmoid_kernel, h, w_gate, (), [], BF16, tm, "proj_merge_gates")

        lam_init = 0.8 - 0.6 * math.exp(-0.3 * l)
        lq = lam_qk[l].astype(F32)
        lam = (jnp.exp(jnp.sum(lq[0] * lq[1])) - jnp.exp(jnp.sum(lq[2] * lq[3])) + lam_init).reshape(1)

        z_ab = z_ab.reshape(b, s, 2 * w_a)
        y_a = _diff_attention(lam, qk.reshape(b, s, 2 * qk_w), v_a.reshape(b, s, w_a), z_ab,
                              diff_norm_g[l], 1.0 - lam_init)
        y_b = _gdn(qkv_b.reshape(b, s, 3 * w_a), gcol.reshape(b, s, LANES),
                   grow.reshape(2 * N_HEADS, b, s // CHUNK, CHUNK), z_ab, conv_w[l], gdn_norm_g[l])

        final = l == depth - 1
        g_next = final_g if final else norm_g[l + 1]
        outs = _merge(y_a.reshape(m, w_a), y_b.reshape(m, w_a), merge_gates, x2,
                      w_branch[l, 0].astype(BF16), w_branch[l, 1].astype(BF16), w_out[l].astype(BF16),
                      g_next, final)
        if final:
            x2 = outs[0]
        else:
            x2, h = outs
    return x2.reshape(b, s, d)


def kernel(x_prompt, x_sample, norm_g, w_in, conv_w, lam_qk, diff_norm_g, a_log, dt_bias, gdn_norm_g, w_branch, w_out, final_g):
    params = (norm_g, w_in, conv_w, lam_qk, diff_norm_g, a_log, dt_bias, gdn_norm_g, w_branch, w_out, final_g)
    return (_trunk(x_prompt, *params), _trunk(x_sample, *params))
```

```python
import functools
import math

import jax
import jax.numpy as jnp
from jax import lax
from jax.experimental import pallas as pl
from jax.experimental.pallas import tpu as pltpu

F32 = jnp.float32
BF16 = jnp.bfloat16

LANES = 128
N_HEADS = 8
HD_QK = 64
HEAD_W = 128
CONV_W = 4
CHUNK = 128
ROPE_THETA = 10000.0
NORM_EPS = 1e-6
SUBLN_EPS = 1e-5
VMEM_LIMIT = 56 * 1024 * 1024
MASKED = -1e30


def _cparams(sem):
    return pltpu.CompilerParams(dimension_semantics=sem, vmem_limit_bytes=VMEM_LIMIT)


def _sigmoid(x):
    return 0.5 * jnp.tanh(0.5 * x) + 0.5


def _silu(x):
    h = 0.5 * x
    return h * jnp.tanh(h) + h


def _rmsnorm_kernel(x_ref, g_ref, o_ref, *, eps):
    x = x_ref[...]
    y = x * lax.rsqrt(jnp.mean(x * x, axis=-1, keepdims=True) + eps)
    o_ref[...] = (y * g_ref[...]).astype(o_ref.dtype)


def _rmsnorm(x, g, out_dtype, tm=1024):
    m, d = x.shape
    tm = min(tm, m)
    return pl.pallas_call(
        functools.partial(_rmsnorm_kernel, eps=NORM_EPS),
        grid=(m // tm,),
        in_specs=[pl.BlockSpec((tm, d), lambda i: (i, 0)), pl.BlockSpec((1, d), lambda i: (0, 0))],
        out_specs=pl.BlockSpec((tm, d), lambda i: (i, 0)),
        out_shape=jax.ShapeDtypeStruct((m, d), out_dtype),
        compiler_params=_cparams(("parallel",)),
        name="rmsnorm",
    )(x, g.reshape(1, d))


PROJ_COLS = 512


def _proj_plain_kernel(h_ref, w_ref, o_ref):
    h = h_ref[...]
    for j in range(o_ref.shape[1] // PROJ_COLS):
        cols = slice(j * PROJ_COLS, (j + 1) * PROJ_COLS)
        o_ref[:, cols] = jnp.dot(h, w_ref[:, cols], preferred_element_type=F32).astype(o_ref.dtype)


def _proj_sigmoid_kernel(h_ref, w_ref, o_ref):
    h = h_ref[...]
    for j in range(o_ref.shape[1] // PROJ_COLS):
        cols = slice(j * PROJ_COLS, (j + 1) * PROJ_COLS)
        acc = jnp.dot(h, w_ref[:, cols], preferred_element_type=F32)
        o_ref[:, cols] = _sigmoid(acc).astype(o_ref.dtype)


def _proj_rope_kernel(h_ref, w_ref, cos_ref, sin_ref, o_ref):
    h = h_ref[...]
    cos = cos_ref[...]
    sin = sin_ref[...]
    lane = lax.broadcasted_iota(jnp.int32, cos.shape, 1)
    first_half = (lane % HD_QK) < (HD_QK // 2)
    for j in range(o_ref.shape[1] // PROJ_COLS):
        acc = jnp.dot(h, w_ref[:, j * PROJ_COLS:(j + 1) * PROJ_COLS], preferred_element_type=F32)
        for g in range(PROJ_COLS // LANES):
            a = acc[:, g * LANES:(g + 1) * LANES]
            rot = jnp.where(first_half, pltpu.roll(a, LANES - HD_QK // 2, 1), pltpu.roll(a, HD_QK // 2, 1))
            lanes = slice(j * PROJ_COLS + g * LANES, j * PROJ_COLS + (g + 1) * LANES)
            o_ref[:, lanes] = (a * cos + rot * sin).astype(o_ref.dtype)


def _proj(kernel_fn, h, w, extras, extra_specs, out_dtype, tm, name):
    m, d = h.shape
    n = w.shape[1]
    return pl.pallas_call(
        kernel_fn,
        grid=(m // tm,),
        in_specs=[pl.BlockSpec((tm, d), lambda i: (i, 0)), pl.BlockSpec((d, n), lambda i: (0, 0))] + extra_specs,
        out_specs=pl.BlockSpec((tm, n), lambda i: (i, 0)),
        out_shape=jax.ShapeDtypeStruct((m, n), out_dtype),
        compiler_params=_cparams(("parallel",)),
        name=name,
    )(h, w, *extras)


def _proj_gdn_gate_kernel(h_ref, w_ref, alog_ref, dtb_ref, col_ref, row_ref):
    acc = jnp.dot(h_ref[...], w_ref[...], preferred_element_type=F32)
    lane = lax.broadcasted_iota(jnp.int32, acc.shape, 1)
    x = acc + dtb_ref[...]
    softplus = jnp.maximum(x, 0.0) + jnp.log(1.0 + jnp.exp(-jnp.abs(x)))
    g = jnp.where(lane < 2 * N_HEADS, (-math.log2(math.e)) * jnp.exp(alog_ref[...]) * softplus, 0.0)
    beta = _sigmoid(acc)
    r = lax.broadcasted_iota(jnp.int32, (CHUNK, CHUNK), 0)
    c = lax.broadcasted_iota(jnp.int32, (CHUNK, CHUNK), 1)
    tril = jnp.where(r >= c, 1.0, 0.0).astype(F32)
    lane_c = lax.broadcasted_iota(jnp.int32, (CHUNK, LANES), 1)
    for ci in range(acc.shape[0] // CHUNK):
        rows = slice(ci * CHUNK, (ci + 1) * CHUNK)
        g_c = g[rows]
        prefix = jnp.dot(tril, g_c, preferred_element_type=F32, precision=lax.Precision.HIGHEST)
        suffix = prefix[CHUNK - 1:CHUNK, :] - prefix + g_c
        out = jnp.where(lane_c < N_HEADS, prefix, jnp.where(lane_c < 2 * N_HEADS, suffix, beta[rows]))
        col_ref[rows, :] = out
        row_ref[:, rows] = out.T[:2 * N_HEADS, :]


def _proj_gdn_gates(h, w, alog_vec, dtb_vec, tm):
    m, d = h.shape
    vec = pl.BlockSpec((1, LANES), lambda i: (0, 0))
    return pl.pallas_call(
        _proj_gdn_gate_kernel,
        grid=(m // tm,),
        in_specs=[pl.BlockSpec((tm, d), lambda i: (i, 0)), pl.BlockSpec((d, LANES), lambda i: (0, 0)), vec, vec],
        out_specs=[pl.BlockSpec((tm, LANES), lambda i: (i, 0)), pl.BlockSpec((2 * N_HEADS, tm), lambda i: (0, i))],
        out_shape=[jax.ShapeDtypeStruct((m, LANES), F32), jax.ShapeDtypeStruct((2 * N_HEADS, m), F32)],
        compiler_params=_cparams(("parallel",)),
        name="proj_delta_gates",
    )(h, w, alog_vec, dtb_vec)


ATTN_ROW_BLOCKS = 4


def _attn_kernel(lam_ref, q_ref, k_ref, v_ref, z_ref, g_ref, o_ref, s0_ref, s1_ref, acc_ref, m_ref, *, tk, post_scale):
    tq = q_ref.shape[1]
    s_len = k_ref.shape[1]
    nk = s_len // tk
    nrb = ATTN_ROW_BLOCKS
    rb = 2 * tq // nrb
    q = q_ref[0]
    lane = lax.broadcasted_iota(jnp.int32, q.shape, 1)
    zero = jnp.zeros_like(q)
    q2 = jnp.concatenate([jnp.where(lane < HD_QK, q, zero), jnp.where(lane >= HD_QK, q, zero)], axis=0)
    ones = jnp.ones((tk, HEAD_W), BF16)

    def scores(kt, s_ref, r):
        rows = slice(r * rb, (r + 1) * rb)
        k = k_ref[0, pl.ds(pl.multiple_of(kt * tk, tk), tk), :]
        s_ref[rows, :] = lax.dot_general(q2[rows], k, (((1,), (1,)), ((), ())), preferred_element_type=F32)

    def update(kt, s_ref, r):
        rows = slice(r * rb, (r + 1) * rb)
        s = s_ref[rows, :]
        m_prev = m_ref[rows, :]
        m_new = jnp.maximum(m_prev, jnp.max(s, axis=1, keepdims=True))
        alpha = jnp.exp2(m_prev - m_new)
        p = jnp.exp2(s - m_new).astype(BF16)
        v_aug = jnp.concatenate([v_ref[0, pl.ds(pl.multiple_of(kt * tk, tk), tk), :], ones], axis=1)
        acc_ref[rows, :] = alpha * acc_ref[rows, :] + jnp.dot(p, v_aug, preferred_element_type=F32)
        m_ref[rows, :] = m_new

    def step(kt_next, s_next, kt, s_cur):
        for r in range(nrb):
            if kt_next is not None:
                scores(kt_next, s_next, r)
            update(kt, s_cur, r)

    m_ref[...] = jnp.full(m_ref.shape, -jnp.inf, F32)
    acc_ref[...] = jnp.zeros(acc_ref.shape, F32)
    for r in range(nrb):
        scores(0, s0_ref, r)

    def body(j, _):
        kt = 2 * j
        step(kt + 1, s1_ref, kt, s0_ref)
        step(kt + 2, s0_ref, kt + 1, s1_ref)
        return 0

    lax.fori_loop(0, nk // 2 - 1, body, 0)
    step(nk - 1, s1_ref, nk - 2, s0_ref)
    step(None, None, nk - 1, s1_ref)
    acc = acc_ref[...]
    o2 = acc[:, :HEAD_W] / acc[:, HEAD_W:]
    o = o2[:tq] - lam_ref[0] * o2[tq:]
    y = o * lax.rsqrt(jnp.mean(o * o, axis=-1, keepdims=True) + SUBLN_EPS) * g_ref[...] * post_scale
    o_ref[0] = (y * _silu(z_ref[0].astype(F32))).astype(o_ref.dtype)


def _diff_attention(lam, qk, v, z, gain, post_scale, tq=1024, tk=2048):
    b, s, w = v.shape
    nh = w // HEAD_W
    tq = min(tq, s)
    tk = min(tk, s // 2)
    assert s % (2 * tk) == 0 and s % tq == 0
    return pl.pallas_call(
        functools.partial(_attn_kernel, tk=tk, post_scale=post_scale),
        scratch_shapes=[
            pltpu.VMEM((2 * tq, tk), F32),
            pltpu.VMEM((2 * tq, tk), F32),
            pltpu.VMEM((2 * tq, 2 * HEAD_W), F32),
            pltpu.VMEM((2 * tq, 1), F32),
        ],
        grid=(b, nh, s // tq),
        in_specs=[
            pl.BlockSpec(memory_space=pltpu.SMEM),
            pl.BlockSpec((1, tq, HEAD_W), lambda bi, hi, qi: (bi, qi, hi)),
            pl.BlockSpec((1, s, HEAD_W), lambda bi, hi, qi: (bi, 0, nh + hi)),
            pl.BlockSpec((1, s, HEAD_W), lambda bi, hi, qi: (bi, 0, hi)),
            pl.BlockSpec((1, tq, HEAD_W), lambda bi, hi, qi: (bi, qi, hi)),
            pl.BlockSpec((1, HEAD_W), lambda bi, hi, qi: (0, 0)),
        ],
        out_specs=pl.BlockSpec((1, tq, HEAD_W), lambda bi, hi, qi: (bi, qi, hi)),
        out_shape=jax.ShapeDtypeStruct((b, s, w), BF16),
        compiler_params=_cparams(("parallel", "parallel", "arbitrary")),
        name="diff_attention",
    )(lam, qk, qk, v, z, gain.reshape(1, HEAD_W))


def _dot(a, b):
    return jnp.dot(a.astype(BF16), b.astype(BF16), preferred_element_type=F32)


def _dot_nt(a, b):
    return lax.dot_general(a.astype(BF16), b.astype(BF16), (((1,), (1,)), ((), ())), preferred_element_type=F32)


INV_BLOCK = 64
LOCAL_UNROLL = 4
STATE_UNROLL = 16


def _unit_triangular_inverses(mats):
    n = mats[0].shape[0]
    row = lax.broadcasted_iota(jnp.int32, (n, n), 0)
    col = lax.broadcasted_iota(jnp.int32, (n, n), 1)
    same_block = (row // INV_BLOCK) == (col // INV_BLOCK)
    eye = jnp.where(row == col, 1.0, 0.0).astype(F32)
    a_diag = [jnp.where(same_block, a, 0.0) for a in mats]
    a_off = [jnp.where(same_block, 0.0, a) for a in mats]
    def mm(a16, b16):
        return jnp.dot(a16, b16, preferred_element_type=F32)

    ts = [eye - a for a in a_diag]
    a16 = [a.astype(BF16) for a in a_diag]
    qs = [mm(a, a) for a in a16]
    levels = int(math.log2(INV_BLOCK)) - 1
    for level in range(levels):
        q16 = [q.astype(BF16) for q in qs]
        t16 = [t.astype(BF16) for t in ts]
        if level < levels - 1:
            prods = [mm(q, jnp.concatenate([q, t], axis=1)) for q, t in zip(q16, t16)]
            qs = [p[:, :n] for p in prods]
            ts = [t + p[:, n:] for t, p in zip(ts, prods)]
        else:
            prods = [mm(q, t) for q, t in zip(q16, t16)]
            ts = [t + p for t, p in zip(ts, prods)]
    t16 = [t.astype(BF16) for t in ts]
    offs = [mm(a.astype(BF16), t) for a, t in zip(a_off, t16)]
    corr = [mm(t, o.astype(BF16)) for t, o in zip(t16, offs)]
    return [t - c for t, c in zip(ts, corr)]


def _conv_silu(x_ref, w_ref, blk, rows, n_blk):
    halo = 16
    t0 = pl.multiple_of(blk * rows, rows)
    xc = x_ref[0, pl.ds(t0, rows), :].astype(F32)
    prev_start = pl.multiple_of(jnp.maximum(t0 - halo, 0), halo)
    next_start = pl.multiple_of(jnp.minimum(t0 + rows, n_blk * rows - halo), halo)
    prev = jnp.where(blk == 0, 0.0, x_ref[0, pl.ds(prev_start, halo), :].astype(F32))
    nxt = jnp.where(blk == n_blk - 1, 0.0, x_ref[0, pl.ds(next_start, halo), :].astype(F32))
    xx = jnp.concatenate([prev, xc, nxt], axis=0)
    n = rows + 2 * halo
    x_m1 = pltpu.roll(xx, 1, 0)[halo:halo + rows]
    x_p1 = pltpu.roll(xx, n - 1, 0)[halo:halo + rows]
    x_p2 = pltpu.roll(xx, n - 2, 0)[halo:halo + rows]
    w = w_ref[...]
    y = x_m1 * w[0:1, :] + xc * w[1:2, :] + x_p1 * w[2:3, :] + x_p2 * w[3:4, :]
    return _silu(y)


def _gdn_kernel(q_ref, k_ref, v_ref, gcol_ref, grow_ref, z_ref, wq_ref, wk_ref, wv_ref, gain_ref, o_ref,
                qkv0_ref, qkv1_ref, u_ref, wqg_ref, sm_ref, sn_ref, attn_ref, egl_ref, dir_ref):
    s_len = q_ref.shape[1]
    n_chunks = s_len // CHUNK
    head = pl.program_id(1)
    row = lax.broadcasted_iota(jnp.int32, (CHUNK, CHUNK), 0)
    col = lax.broadcasted_iota(jnp.int32, (CHUNK, CHUNK), 1)
    lane = lax.broadcasted_iota(jnp.int32, (CHUNK, LANES), 1)

    def conv_group(j, qkv_ref):
        for g in range(LOCAL_UNROLL):
            n = j * LOCAL_UNROLL + g
            rows = pl.ds(g * CHUNK, CHUNK)
            q = _conv_silu(q_ref, wq_ref, n, CHUNK, n_chunks)
            k = _conv_silu(k_ref, wk_ref, n, CHUNK, n_chunks)
            qkv_ref[0, rows, :] = q * (lax.rsqrt(jnp.sum(q * q, axis=-1, keepdims=True) + NORM_EPS) * (HEAD_W ** -0.5))
            qkv_ref[1, rows, :] = k * lax.rsqrt(jnp.sum(k * k, axis=-1, keepdims=True) + NORM_EPS)
            qkv_ref[2, rows, :] = _conv_silu(v_ref, wv_ref, n, CHUNK, n_chunks)

    def local_group(j, qkv_ref):
        ns = [j * LOCAL_UNROLL + g for g in range(LOCAL_UNROLL)]
        t0s = [pl.multiple_of(n * CHUNK, CHUNK) for n in ns]
        load = lambda i, g: qkv_ref[i, pl.ds(g * CHUNK, CHUNK), :]
        qk2s = [_dot_nt(jnp.concatenate([load(0, g), load(1, g)], axis=0), load(1, g))
                for g in range(LOCAL_UNROLL)]
        chains = []
        for g, n in enumerate(ns):
            gates = gcol_ref[0, pl.ds(t0s[g], CHUNK), :]
            for d in range(2):
                sel = head + d * N_HEADS
                gc = jnp.sum(jnp.where(lane == sel, gates, 0.0), axis=1, keepdims=True)
                beta = jnp.sum(jnp.where(lane == sel + 2 * N_HEADS, gates, 0.0), axis=1, keepdims=True)
                gc_row = grow_ref[sel, 0, pl.ds(n, 1), :]
                incl, strict, last = (row >= col, row > col, CHUNK - 1) if d == 0 else (row <= col, row < col, 0)
                decay = jnp.exp2(jnp.where(incl, gc - gc_row, MASKED))
                a = jnp.where(strict, qk2s[g][CHUNK:] * beta * decay, 0.0)
                attn_ref[d, pl.ds(t0s[g], CHUNK), :] = (qk2s[g][:CHUNK] * decay).astype(BF16)
                chains.append((g, d, gc, beta, gc_row[:, last:last + 1], a))
        ts = _unit_triangular_inverses([c[-1] for c in chains])
        egcs = [jnp.exp2(c[2]) for c in chains]
        uws = [_dot(t, jnp.concatenate([load(2, c[0]) * c[3], load(1, c[0]) * (c[3] * egc)], axis=1))
               for t, c, egc in zip(ts, chains, egcs)]
        kdts = [(load(1, c[0]) * jnp.exp2(c[4] - c[2])).T for c in chains]
        trans = [_dot(kdt, uw) for kdt, uw in zip(kdts, uws)]
        for (g, d, gc, beta, gl, _), egc, uw, tr in zip(chains, egcs, uws, trans):
            t0, n = t0s[g], ns[g]
            sn_ref[d, pl.ds(t0, CHUNK), :] = tr[:, :HEAD_W]
            sm_ref[d, pl.ds(t0, CHUNK), :] = tr[:, HEAD_W:].astype(BF16)
            u_ref[d, pl.ds(t0, CHUNK), :] = uw[:, :HEAD_W]
            wqg_ref[d, pl.ds(pl.multiple_of(2 * t0, CHUNK), CHUNK), :] = uw[:, HEAD_W:].astype(BF16)
            wqg_ref[d, pl.ds(pl.multiple_of(2 * t0 + CHUNK, CHUNK), CHUNK), :] = (load(0, g) * egc).astype(BF16)
            egl_ref[d, pl.ds(n, 1), :] = jnp.broadcast_to(jnp.exp2(gl), (1, LANES))

    n_groups = n_chunks // LOCAL_UNROLL
    conv_group(0, qkv0_ref)

    def local_body(i, _):
        j = 2 * i
        conv_group(j + 1, qkv1_ref)
        local_group(j, qkv0_ref)
        conv_group(jnp.minimum(j + 2, n_groups - 1), qkv0_ref)
        local_group(j + 1, qkv1_ref)
        return 0

    lax.fori_loop(0, n_groups // 2, local_body, 0)

    def chunk_starts(i):
        return [pl.multiple_of(n * CHUNK, CHUNK) for n in (i, n_chunks - 1 - i)]

    def emit_outputs(i, v_new, rq):
        t0s = chunk_starts(i)
        intra = [jnp.dot(attn_ref[d, pl.ds(t0s[d], CHUNK), :], v_new[d], preferred_element_type=F32) for d in range(2)]
        for d in range(2):
            dir_ref[d, pl.ds(t0s[d], CHUNK), :] = rq[d] + intra[d]

    def state_step(i, states, pending):
        t0s = chunk_starts(i)
        s16 = [states[d].astype(BF16) for d in range(2)]
        ms = [jnp.dot(sm_ref[d, pl.ds(t0s[d], CHUNK), :], s16[d], preferred_element_type=F32) for d in range(2)]
        rs = [jnp.dot(wqg_ref[d, pl.ds(pl.multiple_of(2 * t0s[d], 2 * CHUNK), 2 * CHUNK), :],
                      s16[d], preferred_element_type=F32) for d in range(2)]
        if pending is not None:
            emit_outputs(*pending)
        new_states = tuple(states[d] * egl_ref[d, pl.ds(n, 1), :] - ms[d] + sn_ref[d, pl.ds(t0s[d], CHUNK), :]
                           for d, n in enumerate((i, n_chunks - 1 - i)))
        v_new = tuple((u_ref[d, pl.ds(t0s[d], CHUNK), :] - rs[d][:CHUNK]).astype(BF16) for d in range(2))
        rq = tuple(rs[d][CHUNK:] for d in range(2))
        return new_states, (i, v_new, rq)

    def state_steps(j, states):
        pending = None
        for g in range(STATE_UNROLL):
            states, pending = state_step(STATE_UNROLL * j + g, states, pending)
        emit_outputs(*pending)
        return states

    zero = jnp.zeros((HEAD_W, HEAD_W), F32)
    lax.fori_loop(0, n_chunks // STATE_UNROLL, state_steps, (zero, zero))

    def finish_body(j, _):
        t0 = pl.multiple_of(j * (LOCAL_UNROLL * CHUNK), LOCAL_UNROLL * CHUNK)
        rows = pl.ds(t0, LOCAL_UNROLL * CHUNK)
        o = dir_ref[0, rows, :] + dir_ref[1, rows, :]
        y = o * lax.rsqrt(jnp.mean(o * o, axis=-1, keepdims=True) + NORM_EPS) * gain_ref[...]
        o_ref[0, rows, :] = (y * _silu(z_ref[0, rows, :].astype(F32))).astype(o_ref.dtype)
        return 0

    lax.fori_loop(0, n_groups, finish_body, 0)


def _gdn(qkv, gcol, grow, z, conv_w, gain):
    b, s, _ = qkv.shape
    nh = N_HEADS
    n_chunks = s // CHUNK
    assert s % CHUNK == 0 and n_chunks % (2 * LOCAL_UNROLL) == 0 and n_chunks % STATE_UNROLL == 0
    seq_spec = lambda off: pl.BlockSpec((1, s, HEAD_W), lambda bi, hi: (bi, 0, hi + off))
    w_spec = lambda off: pl.BlockSpec((CONV_W, HEAD_W), lambda bi, hi: (0, hi + off))
    return pl.pallas_call(
        _gdn_kernel,
        grid=(b, nh),
        in_specs=[
            seq_spec(0), seq_spec(nh), seq_spec(2 * nh),
            pl.BlockSpec((1, s, LANES), lambda bi, hi: (bi, 0, 0)),
            pl.BlockSpec((2 * nh, 1, n_chunks, CHUNK), lambda bi, hi: (0, bi, 0, 0)),
            seq_spec(nh),
            w_spec(0), w_spec(nh), w_spec(2 * nh),
            pl.BlockSpec((1, HEAD_W), lambda bi, hi: (0, 0)),
        ],
        out_specs=seq_spec(0),
        out_shape=jax.ShapeDtypeStruct((b, s, nh * HEAD_W), BF16),
        scratch_shapes=[
            pltpu.VMEM((3, LOCAL_UNROLL * CHUNK, HEAD_W), F32),
            pltpu.VMEM((3, LOCAL_UNROLL * CHUNK, HEAD_W), F32),
            pltpu.VMEM((2, s, HEAD_W), F32),
            pltpu.VMEM((2, 2 * s, HEAD_W), BF16),
            pltpu.VMEM((2, s, HEAD_W), BF16),
            pltpu.VMEM((2, s, HEAD_W), F32),
            pltpu.VMEM((2, s, CHUNK), BF16),
            pltpu.VMEM((2, n_chunks, LANES), F32),
            pltpu.VMEM((2, s, HEAD_W), F32),
        ],
        compiler_params=_cparams(("parallel", "arbitrary")),
        name="gated_delta",
    )(qkv, qkv, qkv, gcol, grow, z, conv_w, conv_w, conv_w, gain.reshape(1, HEAD_W))


def _merge_kernel(ya_ref, yb_ref, gate_ref, x_ref, wa_ref, wb_ref, wo_ref, g_ref, *out_refs, final, tn):
    d = x_ref.shape[1]
    merged = []
    for j in range(d // tn):
        sl = slice(j * tn, (j + 1) * tn)
        pa = jnp.dot(ya_ref[...], wa_ref[:, sl], preferred_element_type=F32)
        pb = jnp.dot(yb_ref[...], wb_ref[:, sl], preferred_element_type=F32)
        ga = gate_ref[:, j * tn:(j + 1) * tn].astype(F32)
        gb = gate_ref[:, d + j * tn:d + (j + 1) * tn].astype(F32)
        merged.append((ga * pa + gb * pb).astype(BF16))
    merged = jnp.concatenate(merged, axis=1)
    x = x_ref[...] + jnp.dot(merged, wo_ref[...], preferred_element_type=F32)
    y = x * lax.rsqrt(jnp.mean(x * x, axis=-1, keepdims=True) + NORM_EPS) * g_ref[...]
    if final:
        out_refs[0][...] = y
    else:
        out_refs[0][...] = x
        out_refs[1][...] = y.astype(BF16)


def _merge(ya, yb, gates, x, wa, wb, wo, g_next, final, tm=512, tn=256):
    m, d = x.shape
    tm = min(tm, m)
    row = lambda width: pl.BlockSpec((tm, width), lambda i: (i, 0))
    full = lambda r, c: pl.BlockSpec((r, c), lambda i: (0, 0))
    if final:
        out_shape = [jax.ShapeDtypeStruct((m, d), F32)]
        out_specs = [row(d)]
    else:
        out_shape = [jax.ShapeDtypeStruct((m, d), F32), jax.ShapeDtypeStruct((m, d), BF16)]
        out_specs = [row(d), row(d)]
    return pl.pallas_call(
        functools.partial(_merge_kernel, final=final, tn=tn),
        grid=(m // tm,),
        in_specs=[row(d), row(d), row(2 * d), row(d), full(d, d), full(d, d), full(d, d), full(1, d)],
        out_specs=out_specs,
        out_shape=out_shape,
        compiler_params=_cparams(("parallel",)),
        name="merge_out",
    )(ya, yb, gates, x, wa, wb, wo, g_next.reshape(1, d))


def _rope_tables(s):
    inv = ROPE_THETA ** (-jnp.arange(0, HD_QK, 2, dtype=F32) / HD_QK)
    ang = jnp.arange(s, dtype=F32)[:, None] * inv[None, :]
    ang = jnp.concatenate([ang, ang], -1)
    sign = jnp.where(jnp.arange(HD_QK) < HD_QK // 2, -1.0, 1.0).astype(F32)
    cos = jnp.tile(jnp.cos(ang), (1, LANES // HD_QK))
    sin = jnp.tile(jnp.sin(ang) * sign[None, :], (1, LANES // HD_QK))
    return cos, sin


def _trunk(x, norm_g, w_in, conv_w, lam_qk, diff_norm_g, a_log, dt_bias, gdn_norm_g, w_branch, w_out, final_g):
    b, s, d = x.shape
    depth = w_in.shape[0]
    m = b * s
    w_a = N_HEADS * HEAD_W
    qk_w = N_HEADS * 2 * HD_QK
    c_q, c_k, c_v, c_za = 0, qk_w, 2 * qk_w, 2 * qk_w + w_a
    c_qkvb = c_za + w_a
    c_zb = c_qkvb + 3 * w_a
    c_a = c_zb + w_a
    c_b = c_a + 2 * N_HEADS
    c_gate = c_b + 2 * N_HEADS

    cos, sin = _rope_tables(s)
    tm = min(1024, s)
    pos_blocks = s // tm
    rope_specs = [pl.BlockSpec((tm, LANES), lambda i: (i % pos_blocks, 0))] * 2

    x2 = x.reshape(m, d)
    h = _rmsnorm(x2, norm_g[0], BF16)
    for l in range(depth):
        w = w_in[l]
        scale = HD_QK ** -0.5 * math.log2(math.e)
        w_qk = jnp.concatenate([w[:, c_q:c_k] * scale, w[:, c_k:c_v]], axis=1).astype(BF16)
        w_v = w[:, c_v:c_za].astype(BF16)
        w_z = jnp.concatenate([w[:, c_za:c_qkvb], w[:, c_zb:c_a]], axis=1).astype(BF16)
        w_qkvb = w[:, c_qkvb:c_zb].astype(BF16)
        w_ab = jnp.pad(w[:, c_a:c_gate], ((0, 0), (0, LANES - 4 * N_HEADS))).astype(BF16)
        w_gate = w[:, c_gate:].astype(BF16)
        alog_vec = jnp.pad(a_log[l].reshape(1, -1), ((0, 0), (0, LANES - 2 * N_HEADS)))
        dtb_vec = jnp.pad(dt_bias[l].reshape(1, -1), ((0, 0), (0, LANES - 2 * N_HEADS)))

        qk = _proj(_proj_rope_kernel, h, w_qk, (cos, sin), rope_specs, BF16, tm, "proj_qk_rope")
        v_a = _proj(_proj_plain_kernel, h, w_v, (), [], BF16, tm, "proj_v")
        z_ab = _proj(_proj_plain_kernel, h, w_z, (), [], BF16, tm, "proj_z")
        qkv_b = _proj(_proj_plain_kernel, h, w_qkvb, (), [], BF16, tm, "proj_qkv_delta")
        gcol, grow = _proj_gdn_gates(h, w_ab, alog_vec, dtb_vec, tm)
        merge_gates = _proj(_proj_sigmoid_kernel, h, w_gate, (), [], BF16, tm, "proj_merge_gates")

        lam_init = 0.8 - 0.6 * math.exp(-0.3 * l)
        lq = lam_qk[l].astype(F32)
        lam = (jnp.exp(jnp.sum(lq[0] * lq[1])) - jnp.exp(jnp.sum(lq[2] * lq[3])) + lam_init).reshape(1)

        z_ab = z_ab.reshape(b, s, 2 * w_a)
        y_a = _diff_attention(lam, qk.reshape(b, s, 2 * qk_w), v_a.reshape(b, s, w_a), z_ab,
                              diff_norm_g[l], 1.0 - lam_init)
        y_b = _gdn(qkv_b.reshape(b, s, 3 * w_a), gcol.reshape(b, s, LANES),
                   grow.reshape(2 * N_HEADS, b, s // CHUNK, CHUNK), z_ab, conv_w[l], gdn_norm_g[l])

        final = l == depth - 1
        g_next = final_g if final else norm_g[l + 1]
        outs = _merge(y_a.reshape(m, w_a), y_b.reshape(m, w_a), merge_gates, x2,
                      w_branch[l, 0].astype(BF16), w_branch[l, 1].astype(BF16), w_out[l].astype(BF16),
                      g_next, final)
        if final:
            x2 = outs[0]
        else:
            x2, h = outs
    return x2.reshape(b, s, d)


def kernel(x_prompt, x_sample, norm_g, w_in, conv_w, lam_qk, diff_norm_g, a_log, dt_bias, gdn_norm_g, w_branch, w_out, final_g):
    params = (norm_g, w_in, conv_w, lam_qk, diff_norm_g, a_log, dt_bias, gdn_norm_g, w_branch, w_out, final_g)
    return (_trunk(x_prompt, *params), _trunk(x_sample, *params))
```

```python
import functools
import math

import jax
import jax.numpy as jnp
from jax import lax
from jax.experimental import pallas as pl
from jax.experimental.pallas import tpu as pltpu

F32 = jnp.float32
BF16 = jnp.bfloat16

LANES = 128
N_HEADS = 8
HD_QK = 64
HEAD_W = 128
CONV_W = 4
CHUNK = 128
ROPE_THETA = 10000.0
NORM_EPS = 1e-6
SUBLN_EPS = 1e-5
VMEM_LIMIT = 56 * 1024 * 1024
MASKED = -1e30


def _cparams(sem):
    return pltpu.CompilerParams(dimension_semantics=sem, vmem_limit_bytes=VMEM_LIMIT)


def _sigmoid(x):
    return 0.5 * jnp.tanh(0.5 * x) + 0.5


def _silu(x):
    h = 0.5 * x
    return h * jnp.tanh(h) + h


def _rmsnorm_kernel(x_ref, g_ref, o_ref, *, eps):
    x = x_ref[...]
    y = x * lax.rsqrt(jnp.mean(x * x, axis=-1, keepdims=True) + eps)
    o_ref[...] = (y * g_ref[...]).astype(o_ref.dtype)


def _rmsnorm(x, g, out_dtype, tm=1024):
    m, d = x.shape
    tm = min(tm, m)
    return pl.pallas_call(
        functools.partial(_rmsnorm_kernel, eps=NORM_EPS),
        grid=(m // tm,),
        in_specs=[pl.BlockSpec((tm, d), lambda i: (i, 0)), pl.BlockSpec((1, d), lambda i: (0, 0))],
        out_specs=pl.BlockSpec((tm, d), lambda i: (i, 0)),
        out_shape=jax.ShapeDtypeStruct((m, d), out_dtype),
        compiler_params=_cparams(("parallel",)),
        name="rmsnorm",
    )(x, g.reshape(1, d))


PROJ_COLS = 512


def _proj_plain_kernel(h_ref, w_ref, o_ref):
    h = h_ref[...]
    for j in range(o_ref.shape[1] // PROJ_COLS):
        cols = slice(j * PROJ_COLS, (j + 1) * PROJ_COLS)
        o_ref[:, cols] = jnp.dot(h, w_ref[:, cols], preferred_element_type=F32).astype(o_ref.dtype)


def _proj_sigmoid_kernel(h_ref, w_ref, o_ref):
    h = h_ref[...]
    for j in range(o_ref.shape[1] // PROJ_COLS):
        cols = slice(j * PROJ_COLS, (j + 1) * PROJ_COLS)
        acc = jnp.dot(h, w_ref[:, cols], preferred_element_type=F32)
        o_ref[:, cols] = _sigmoid(acc).astype(o_ref.dtype)


def _proj_rope_kernel(h_ref, w_ref, cos_ref, sin_ref, o_ref):
    h = h_ref[...]
    cos = cos_ref[...]
    sin = sin_ref[...]
    lane = lax.broadcasted_iota(jnp.int32, cos.shape, 1)
    first_half = (lane % HD_QK) < (HD_QK // 2)
    for j in range(o_ref.shape[1] // PROJ_COLS):
        acc = jnp.dot(h, w_ref[:, j * PROJ_COLS:(j + 1) * PROJ_COLS], preferred_element_type=F32)
        for g in range(PROJ_COLS // LANES):
            a = acc[:, g * LANES:(g + 1) * LANES]
            rot = jnp.where(first_half, pltpu.roll(a, LANES - HD_QK // 2, 1), pltpu.roll(a, HD_QK // 2, 1))
            lanes = slice(j * PROJ_COLS + g * LANES, j * PROJ_COLS + (g + 1) * LANES)
            o_ref[:, lanes] = (a * cos + rot * sin).astype(o_ref.dtype)


def _proj(kernel_fn, h, w, extras, extra_specs, out_dtype, tm, name):
    m, d = h.shape
    n = w.shape[1]
    return pl.pallas_call(
        kernel_fn,
        grid=(m // tm,),
        in_specs=[pl.BlockSpec((tm, d), lambda i: (i, 0)), pl.BlockSpec((d, n), lambda i: (0, 0))] + extra_specs,
        out_specs=pl.BlockSpec((tm, n), lambda i: (i, 0)),
        out_shape=jax.ShapeDtypeStruct((m, n), out_dtype),
        compiler_params=_cparams(("parallel",)),
        name=name,
    )(h, w, *extras)


def _proj_gdn_gate_kernel(h_ref, w_ref, alog_ref, dtb_ref, col_ref, row_ref):
    acc = jnp.dot(h_ref[...], w_ref[...], preferred_element_type=F32)
    lane = lax.broadcasted_iota(jnp.int32, acc.shape, 1)
    x = acc + dtb_ref[...]
    softplus = jnp.maximum(x, 0.0) + jnp.log(1.0 + jnp.exp(-jnp.abs(x)))
    g = jnp.where(lane < 2 * N_HEADS, (-math.log2(math.e)) * jnp.exp(alog_ref[...]) * softplus, 0.0)
    beta = _sigmoid(acc)
    r = lax.broadcasted_iota(jnp.int32, (CHUNK, CHUNK), 0)
    c = lax.broadcasted_iota(jnp.int32, (CHUNK, CHUNK), 1)
    tril = jnp.where(r >= c, 1.0, 0.0).astype(F32)
    lane_c = lax.broadcasted_iota(jnp.int32, (CHUNK, LANES), 1)
    for ci in range(acc.shape[0] // CHUNK):
        rows = slice(ci * CHUNK, (ci + 1) * CHUNK)
        g_c = g[rows]
        prefix = jnp.dot(tril, g_c, preferred_element_type=F32, precision=lax.Precision.HIGHEST)
        suffix = prefix[CHUNK - 1:CHUNK, :] - prefix + g_c
        out = jnp.where(lane_c < N_HEADS, prefix, jnp.where(lane_c < 2 * N_HEADS, suffix, beta[rows]))
        col_ref[rows, :] = out
        row_ref[:, rows] = out.T[:2 * N_HEADS, :]


def _proj_gdn_gates(h, w, alog_vec, dtb_vec, tm):
    m, d = h.shape
    vec = pl.BlockSpec((1, LANES), lambda i: (0, 0))
    return pl.pallas_call(
        _proj_gdn_gate_kernel,
        grid=(m // tm,),
        in_specs=[pl.BlockSpec((tm, d), lambda i: (i, 0)), pl.BlockSpec((d, LANES), lambda i: (0, 0)), vec, vec],
        out_specs=[pl.BlockSpec((tm, LANES), lambda i: (i, 0)), pl.BlockSpec((2 * N_HEADS, tm), lambda i: (0, i))],
        out_shape=[jax.ShapeDtypeStruct((m, LANES), F32), jax.ShapeDtypeStruct((2 * N_HEADS, m), F32)],
        compiler_params=_cparams(("parallel",)),
        name="proj_delta_gates",
    )(h, w, alog_vec, dtb_vec)


ATTN_ROW_BLOCKS = 4


def _attn_kernel(lam_ref, q_ref, k_ref, v_ref, z_ref, g_ref, o_ref, s0_ref, s1_ref, acc_ref, m_ref, *, tk, post_scale):
    tq = q_ref.shape[1]
    s_len = k_ref.shape[1]
    nk = s_len // tk
    nrb = ATTN_ROW_BLOCKS
    rb = 2 * tq // nrb
    q = q_ref[0]
    lane = lax.broadcasted_iota(jnp.int32, q.shape, 1)
    zero = jnp.zeros_like(q)
    q2 = jnp.concatenate([jnp.where(lane < HD_QK, q, zero), jnp.where(lane >= HD_QK, q, zero)], axis=0)
    ones = jnp.ones((tk, HEAD_W), BF16)

    def scores(kt, s_ref, r):
        rows = slice(r * rb, (r + 1) * rb)
        k = k_ref[0, pl.ds(pl.multiple_of(kt * tk, tk), tk), :]
        s_ref[rows, :] = lax.dot_general(q2[rows], k, (((1,), (1,)), ((), ())), preferred_element_type=F32)

    def update(kt, s_ref, r):
        rows = slice(r * rb, (r + 1) * rb)
        s = s_ref[rows, :]
        m_prev = m_ref[rows, :]
        m_new = jnp.maximum(m_prev, jnp.max(s, axis=1, keepdims=True))
        alpha = jnp.exp2(m_prev - m_new)
        p = jnp.exp2(s - m_new).astype(BF16)
        v_aug = jnp.concatenate([v_ref[0, pl.ds(pl.multiple_of(kt * tk, tk), tk), :], ones], axis=1)
        acc_ref[rows, :] = alpha * acc_ref[rows, :] + jnp.dot(p, v_aug, preferred_element_type=F32)
        m_ref[rows, :] = m_new

    def step(kt_next, s_next, kt, s_cur):
        for r in range(nrb):
            if kt_next is not None:
                scores(kt_next, s_next, r)
            update(kt, s_cur, r)

    m_ref[...] = jnp.full(m_ref.shape, -jnp.inf, F32)
    acc_ref[...] = jnp.zeros(acc_ref.shape, F32)
    for r in range(nrb):
        scores(0, s0_ref, r)

    def body(j, _):
        kt = 2 * j
        step(kt + 1, s1_ref, kt, s0_ref)
        step(kt + 2, s0_ref, kt + 1, s1_ref)
        return 0

    lax.fori_loop(0, nk // 2 - 1, body, 0)
    step(nk - 1, s1_ref, nk - 2, s0_ref)
    step(None, None, nk - 1, s1_ref)
    acc = acc_ref[...]
    o2 = acc[:, :HEAD_W] / acc[:, HEAD_W:]
    o = o2[:tq] - lam_ref[0] * o2[tq:]
    y = o * lax.rsqrt(jnp.mean(o * o, axis=-1, keepdims=True) + SUBLN_EPS) * g_ref[...] * post_scale
    o_ref[0] = (y * _silu(z_ref[0].astype(F32))).astype(o_ref.dtype)


def _diff_attention(lam, qk, v, z, gain, post_scale, tq=1024, tk=2048):
    b, s, w = v.shape
    nh = w // HEAD_W
    tq = min(tq, s)
    tk = min(tk, s // 2)
    assert s % (2 * tk) == 0 and s % tq == 0
    return pl.pallas_call(
        functools.partial(_attn_kernel, tk=tk, post_scale=post_scale),
        scratch_shapes=[
            pltpu.VMEM((2 * tq, tk), F32),
            pltpu.VMEM((2 * tq, tk), F32),
            pltpu.VMEM((2 * tq, 2 * HEAD_W), F32),
            pltpu.VMEM((2 * tq, 1), F32),
        ],
        grid=(b, nh, s // tq),
        in_specs=[
            pl.BlockSpec(memory_space=pltpu.SMEM),
            pl.BlockSpec((1, tq, HEAD_W), lambda bi, hi, qi: (bi, qi, hi)),
            pl.BlockSpec((1, s, HEAD_W), lambda bi, hi, qi: (bi, 0, nh + hi)),
            pl.BlockSpec((1, s, HEAD_W), lambda bi, hi, qi: (bi, 0, hi)),
            pl.BlockSpec((1, tq, HEAD_W), lambda bi, hi, qi: (bi, qi, hi)),
            pl.BlockSpec((1, HEAD_W), lambda bi, hi, qi: (0, 0)),
        ],
        out_specs=pl.BlockSpec((1, tq, HEAD_W), lambda bi, hi, qi: (bi, qi, hi)),
        out_shape=jax.ShapeDtypeStruct((b, s, w), BF16),
        compiler_params=_cparams(("parallel", "parallel", "arbitrary")),
        name="diff_attention",
    )(lam, qk, qk, v, z, gain.reshape(1, HEAD_W))


def _dot(a, b):
    return jnp.dot(a.astype(BF16), b.astype(BF16), preferred_element_type=F32)


def _dot_nt(a, b):
    return lax.dot_general(a.astype(BF16), b.astype(BF16), (((1,), (1,)), ((), ())), preferred_element_type=F32)


INV_BLOCK = 64
LOCAL_UNROLL = 8
STATE_UNROLL = 16


def _unit_triangular_inverses(mats):
    n = mats[0].shape[0]
    row = lax.broadcasted_iota(jnp.int32, (n, n), 0)
    col = lax.broadcasted_iota(jnp.int32, (n, n), 1)
    same_block = (row // INV_BLOCK) == (col // INV_BLOCK)
    eye = jnp.where(row == col, 1.0, 0.0).astype(F32)
    a_diag = [jnp.where(same_block, a, 0.0) for a in mats]
    a_off = [jnp.where(same_block, 0.0, a) for a in mats]
    def mm(a16, b16):
        return jnp.dot(a16, b16, preferred_element_type=F32)

    ts = [eye - a for a in a_diag]
    a16 = [a.astype(BF16) for a in a_diag]
    qs = [mm(a, a) for a in a16]
    levels = int(math.log2(INV_BLOCK)) - 1
    for level in range(levels):
        q16 = [q.astype(BF16) for q in qs]
        t16 = [t.astype(BF16) for t in ts]
        if level < levels - 1:
            prods = [mm(q, jnp.concatenate([q, t], axis=1)) for q, t in zip(q16, t16)]
            qs = [p[:, :n] for p in prods]
            ts = [t + p[:, n:] for t, p in zip(ts, prods)]
        else:
            prods = [mm(q, t) for q, t in zip(q16, t16)]
            ts = [t + p for t, p in zip(ts, prods)]
    t16 = [t.astype(BF16) for t in ts]
    offs = [mm(a.astype(BF16), t) for a, t in zip(a_off, t16)]
    corr = [mm(t, o.astype(BF16)) for t, o in zip(t16, offs)]
    return [t - c for t, c in zip(ts, corr)]


def _conv_silu(x_ref, w_ref, blk, rows, n_blk):
    halo = 16
    t0 = pl.multiple_of(blk * rows, rows)
    xc = x_ref[0, pl.ds(t0, rows), :].astype(F32)
    prev_start = pl.multiple_of(jnp.maximum(t0 - halo, 0), halo)
    next_start = pl.multiple_of(jnp.minimum(t0 + rows, n_blk * rows - halo), halo)
    prev = jnp.where(blk == 0, 0.0, x_ref[0, pl.ds(prev_start, halo), :].astype(F32))
    nxt = jnp.where(blk == n_blk - 1, 0.0, x_ref[0, pl.ds(next_start, halo), :].astype(F32))
    xx = jnp.concatenate([prev, xc, nxt], axis=0)
    n = rows + 2 * halo
    x_m1 = pltpu.roll(xx, 1, 0)[halo:halo + rows]
    x_p1 = pltpu.roll(xx, n - 1, 0)[halo:halo + rows]
    x_p2 = pltpu.roll(xx, n - 2, 0)[halo:halo + rows]
    w = w_ref[...]
    y = x_m1 * w[0:1, :] + xc * w[1:2, :] + x_p1 * w[2:3, :] + x_p2 * w[3:4, :]
    return _silu(y)


def _gdn_kernel(q_ref, k_ref, v_ref, gcol_ref, grow_ref, z_ref, wq_ref, wk_ref, wv_ref, gain_ref, o_ref,
                qkv0_ref, qkv1_ref, u_ref, wqg_ref, sm_ref, sn_ref, attn_ref, egl_ref, dir_ref):
    s_len = q_ref.shape[1]
    n_chunks = s_len // CHUNK
    head = pl.program_id(1)
    row = lax.broadcasted_iota(jnp.int32, (CHUNK, CHUNK), 0)
    col = lax.broadcasted_iota(jnp.int32, (CHUNK, CHUNK), 1)
    lane = lax.broadcasted_iota(jnp.int32, (CHUNK, LANES), 1)

    def conv_group(j, qkv_ref):
        for g in range(LOCAL_UNROLL):
            n = j * LOCAL_UNROLL + g
            rows = pl.ds(g * CHUNK, CHUNK)
            q = _conv_silu(q_ref, wq_ref, n, CHUNK, n_chunks)
            k = _conv_silu(k_ref, wk_ref, n, CHUNK, n_chunks)
            qkv_ref[0, rows, :] = q * (lax.rsqrt(jnp.sum(q * q, axis=-1, keepdims=True) + NORM_EPS) * (HEAD_W ** -0.5))
            qkv_ref[1, rows, :] = k * lax.rsqrt(jnp.sum(k * k, axis=-1, keepdims=True) + NORM_EPS)
            qkv_ref[2, rows, :] = _conv_silu(v_ref, wv_ref, n, CHUNK, n_chunks)

    def local_group(j, qkv_ref):
        ns = [j * LOCAL_UNROLL + g for g in range(LOCAL_UNROLL)]
        t0s = [pl.multiple_of(n * CHUNK, CHUNK) for n in ns]
        load = lambda i, g: qkv_ref[i, pl.ds(g * CHUNK, CHUNK), :]
        qk2s = [_dot_nt(jnp.concatenate([load(0, g), load(1, g)], axis=0), load(1, g))
                for g in range(LOCAL_UNROLL)]
        chains = []
        for g, n in enumerate(ns):
            gates = gcol_ref[0, pl.ds(t0s[g], CHUNK), :]
            for d in range(2):
                sel = head + d * N_HEADS
                gc = jnp.sum(jnp.where(lane == sel, gates, 0.0), axis=1, keepdims=True)
                beta = jnp.sum(jnp.where(lane == sel + 2 * N_HEADS, gates, 0.0), axis=1, keepdims=True)
                gc_row = grow_ref[sel, 0, pl.ds(n, 1), :]
                incl, strict, last = (row >= col, row > col, CHUNK - 1) if d == 0 else (row <= col, row < col, 0)
                decay = jnp.exp2(jnp.where(incl, gc - gc_row, MASKED))
                a = jnp.where(strict, qk2s[g][CHUNK:] * beta * decay, 0.0)
                attn_ref[d, pl.ds(t0s[g], CHUNK), :] = (qk2s[g][:CHUNK] * decay).astype(BF16)
                chains.append((g, d, gc, beta, gc_row[:, last:last + 1], a))
        ts = _unit_triangular_inverses([c[-1] for c in chains])
        egcs = [jnp.exp2(c[2]) for c in chains]
        uws = [_dot(t, jnp.concatenate([load(2, c[0]) * c[3], load(1, c[0]) * (c[3] * egc)], axis=1))
               for t, c, egc in zip(ts, chains, egcs)]
        kdts = [(load(1, c[0]) * jnp.exp2(c[4] - c[2])).T for c in chains]
        trans = [_dot(kdt, uw) for kdt, uw in zip(kdts, uws)]
        for (g, d, gc, beta, gl, _), egc, uw, tr in zip(chains, egcs, uws, trans):
            t0, n = t0s[g], ns[g]
            sn_ref[d, pl.ds(t0, CHUNK), :] = tr[:, :HEAD_W]
            sm_ref[d, pl.ds(t0, CHUNK), :] = tr[:, HEAD_W:].astype(BF16)
            u_ref[d, pl.ds(t0, CHUNK), :] = uw[:, :HEAD_W]
            wqg_ref[d, pl.ds(pl.multiple_of(2 * t0, CHUNK), CHUNK), :] = uw[:, HEAD_W:].astype(BF16)
            wqg_ref[d, pl.ds(pl.multiple_of(2 * t0 + CHUNK, CHUNK), CHUNK), :] = (load(0, g) * egc).astype(BF16)
            egl_ref[d, pl.ds(n, 1), :] = jnp.broadcast_to(jnp.exp2(gl), (1, LANES))

    n_groups = n_chunks // LOCAL_UNROLL
    conv_group(0, qkv0_ref)

    def local_body(i, _):
        j = 2 * i
        conv_group(j + 1, qkv1_ref)
        local_group(j, qkv0_ref)
        conv_group(jnp.minimum(j + 2, n_groups - 1), qkv0_ref)
        local_group(j + 1, qkv1_ref)
        return 0

    lax.fori_loop(0, n_groups // 2, local_body, 0)

    def chunk_starts(i):
        return [pl.multiple_of(n * CHUNK, CHUNK) for n in (i, n_chunks - 1 - i)]

    def emit_outputs(i, v_new, rq):
        t0s = chunk_starts(i)
        intra = [jnp.dot(attn_ref[d, pl.ds(t0s[d], CHUNK), :], v_new[d], preferred_element_type=F32) for d in range(2)]
        for d in range(2):
            dir_ref[d, pl.ds(t0s[d], CHUNK), :] = rq[d] + intra[d]

    def state_step(i, states, pending):
        t0s = chunk_starts(i)
        s16 = [states[d].astype(BF16) for d in range(2)]
        ms = [jnp.dot(sm_ref[d, pl.ds(t0s[d], CHUNK), :], s16[d], preferred_element_type=F32) for d in range(2)]
        rs = [jnp.dot(wqg_ref[d, pl.ds(pl.multiple_of(2 * t0s[d], 2 * CHUNK), 2 * CHUNK), :],
                      s16[d], preferred_element_type=F32) for d in range(2)]
        if pending is not None:
            emit_outputs(*pending)
        new_states = tuple(states[d] * egl_ref[d, pl.ds(n, 1), :] - ms[d] + sn_ref[d, pl.ds(t0s[d], CHUNK), :]
                           for d, n in enumerate((i, n_chunks - 1 - i)))
        v_new = tuple((u_ref[d, pl.ds(t0s[d], CHUNK), :] - rs[d][:CHUNK]).astype(BF16) for d in range(2))
        rq = tuple(rs[d][CHUNK:] for d in range(2))
        return new_states, (i, v_new, rq)

    def state_steps(j, states):
        pending = None
        for g in range(STATE_UNROLL):
            states, pending = state_step(STATE_UNROLL * j + g, states, pending)
        emit_outputs(*pending)
        return states

    zero = jnp.zeros((HEAD_W, HEAD_W), F32)
    lax.fori_loop(0, n_chunks // STATE_UNROLL, state_steps, (zero, zero))

    def finish_body(j, _):
        t0 = pl.multiple_of(j * (LOCAL_UNROLL * CHUNK), LOCAL_UNROLL * CHUNK)
        rows = pl.ds(t0, LOCAL_UNROLL * CHUNK)
        o = dir_ref[0, rows, :] + dir_ref[1, rows, :]
        y = o * lax.rsqrt(jnp.mean(o * o, axis=-1, keepdims=True) + NORM_EPS) * gain_ref[...]
        o_ref[0, rows, :] = (y * _silu(z_ref[0, rows, :].astype(F32))).astype(o_ref.dtype)
        return 0

    lax.fori_loop(0, n_groups, finish_body, 0)


def _gdn(qkv, gcol, grow, z, conv_w, gain):
    b, s, _ = qkv.shape
    nh = N_HEADS
    n_chunks = s // CHUNK
    assert s % CHUNK == 0 and n_chunks % (2 * LOCAL_UNROLL) == 0 and n_chunks % STATE_UNROLL == 0
    seq_spec = lambda off: pl.BlockSpec((1, s, HEAD_W), lambda bi, hi: (bi, 0, hi + off))
    w_spec = lambda off: pl.BlockSpec((CONV_W, HEAD_W), lambda bi, hi: (0, hi + off))
    return pl.pallas_call(
        _gdn_kernel,
        grid=(b, nh),
        in_specs=[
            seq_spec(0), seq_spec(nh), seq_spec(2 * nh),
            pl.BlockSpec((1, s, LANES), lambda bi, hi: (bi, 0, 0)),
            pl.BlockSpec((2 * nh, 1, n_chunks, CHUNK), lambda bi, hi: (0, bi, 0, 0)),
            seq_spec(nh),
            w_spec(0), w_spec(nh), w_spec(2 * nh),
            pl.BlockSpec((1, HEAD_W), lambda bi, hi: (0, 0)),
        ],
        out_specs=seq_spec(0),
        out_shape=jax.ShapeDtypeStruct((b, s, nh * HEAD_W), BF16),
        scratch_shapes=[
            pltpu.VMEM((3, LOCAL_UNROLL * CHUNK, HEAD_W), F32),
            pltpu.VMEM((3, LOCAL_UNROLL * CHUNK, HEAD_W), F32),
            pltpu.VMEM((2, s, HEAD_W), F32),
            pltpu.VMEM((2, 2 * s, HEAD_W), BF16),
            pltpu.VMEM((2, s, HEAD_W), BF16),
            pltpu.VMEM((2, s, HEAD_W), F32),
            pltpu.VMEM((2, s, CHUNK), BF16),
            pltpu.VMEM((2, n_chunks, LANES), F32),
            pltpu.VMEM((2, s, HEAD_W), F32),
        ],
        compiler_params=_cparams(("parallel", "arbitrary")),
        name="gated_delta",
    )(qkv, qkv, qkv, gcol, grow, z, conv_w, conv_w, conv_w, gain.reshape(1, HEAD_W))


def _merge_kernel(ya_ref, yb_ref, gate_ref, x_ref, wa_ref, wb_ref, wo_ref, g_ref, *out_refs, final, tn):
    d = x_ref.shape[1]
    merged = []
    for j in range(d // tn):
        sl = slice(j * tn, (j + 1) * tn)
        pa = jnp.dot(ya_ref[...], wa_ref[:, sl], preferred_element_type=F32)
        pb = jnp.dot(yb_ref[...], wb_ref[:, sl], preferred_element_type=F32)
        ga = gate_ref[:, j * tn:(j + 1) * tn].astype(F32)
        gb = gate_ref[:, d + j * tn:d + (j + 1) * tn].astype(F32)
        merged.append((ga * pa + gb * pb).astype(BF16))
    merged = jnp.concatenate(merged, axis=1)
    x = x_ref[...] + jnp.dot(merged, wo_ref[...], preferred_element_type=F32)
    y = x * lax.rsqrt(jnp.mean(x * x, axis=-1, keepdims=True) + NORM_EPS) * g_ref[...]
    if final:
        out_refs[0][...] = y
    else:
        out_refs[0][...] = x
        out_refs[1][...] = y.astype(BF16)


def _merge(ya, yb, gates, x, wa, wb, wo, g_next, final, tm=512, tn=256):
    m, d = x.shape
    tm = min(tm, m)
    row = lambda width: pl.BlockSpec((tm, width), lambda i: (i, 0))
    full = lambda r, c: pl.BlockSpec((r, c), lambda i: (0, 0))
    if final:
        out_shape = [jax.ShapeDtypeStruct((m, d), F32)]
        out_specs = [row(d)]
    else:
        out_shape = [jax.ShapeDtypeStruct((m, d), F32), jax.ShapeDtypeStruct((m, d), BF16)]
        out_specs = [row(d), row(d)]
    return pl.pallas_call(
        functools.partial(_merge_kernel, final=final, tn=tn),
        grid=(m // tm,),
        in_specs=[row(d), row(d), row(2 * d), row(d), full(d, d), full(d, d), full(d, d), full(1, d)],
        out_specs=out_specs,
        out_shape=out_shape,
        compiler_params=_cparams(("parallel",)),
        name="merge_out",
    )(ya, yb, gates, x, wa, wb, wo, g_next.reshape(1, d))


def _rope_tables(s):
    inv = ROPE_THETA ** (-jnp.arange(0, HD_QK, 2, dtype=F32) / HD_QK)
    ang = jnp.arange(s, dtype=F32)[:, None] * inv[None, :]
    ang = jnp.concatenate([ang, ang], -1)
    sign = jnp.where(jnp.arange(HD_QK) < HD_QK // 2, -1.0, 1.0).astype(F32)
    cos = jnp.tile(jnp.cos(ang), (1, LANES // HD_QK))
    sin = jnp.tile(jnp.sin(ang) * sign[None, :], (1, LANES // HD_QK))
    return cos, sin


def _trunk(x, norm_g, w_in, conv_w, lam_qk, diff_norm_g, a_log, dt_bias, gdn_norm_g, w_branch, w_out, final_g):
    b, s, d = x.shape
    depth = w_in.shape[0]
    m = b * s
    w_a = N_HEADS * HEAD_W
    qk_w = N_HEADS * 2 * HD_QK
    c_q, c_k, c_v, c_za = 0, qk_w, 2 * qk_w, 2 * qk_w + w_a
    c_qkvb = c_za + w_a
    c_zb = c_qkvb + 3 * w_a
    c_a = c_zb + w_a
    c_b = c_a + 2 * N_HEADS
    c_gate = c_b + 2 * N_HEADS

    cos, sin = _rope_tables(s)
    tm = min(1024, s)
    pos_blocks = s // tm
    rope_specs = [pl.BlockSpec((tm, LANES), lambda i: (i % pos_blocks, 0))] * 2

    x2 = x.reshape(m, d)
    h = _rmsnorm(x2, norm_g[0], BF16)
    for l in range(depth):
        w = w_in[l]
        scale = HD_QK ** -0.5 * math.log2(math.e)
        w_qk = jnp.concatenate([w[:, c_q:c_k] * scale, w[:, c_k:c_v]], axis=1).astype(BF16)
        w_v = w[:, c_v:c_za].astype(BF16)
        w_z = jnp.concatenate([w[:, c_za:c_qkvb], w[:, c_zb:c_a]], axis=1).astype(BF16)
        w_qkvb = w[:, c_qkvb:c_zb].astype(BF16)
        w_ab = jnp.pad(w[:, c_a:c_gate], ((0, 0), (0, LANES - 4 * N_HEADS))).astype(BF16)
        w_gate = w[:, c_gate:].astype(BF16)
        alog_vec = jnp.pad(a_log[l].reshape(1, -1), ((0, 0), (0, LANES - 2 * N_HEADS)))
        dtb_vec = jnp.pad(dt_bias[l].reshape(1, -1), ((0, 0), (0, LANES - 2 * N_HEADS)))

        qk = _proj(_proj_rope_kernel, h, w_qk, (cos, sin), rope_specs, BF16, tm, "proj_qk_rope")
        v_a = _proj(_proj_plain_kernel, h, w_v, (), [], BF16, tm, "proj_v")
        z_ab = _proj(_proj_plain_kernel, h, w_z, (), [], BF16, tm, "proj_z")
        qkv_b = _proj(_proj_plain_kernel, h, w_qkvb, (), [], BF16, tm, "proj_qkv_delta")
        gcol, grow = _proj_gdn_gates(h, w_ab, alog_vec, dtb_vec, tm)
        merge_gates = _proj(_proj_sigmoid_kernel, h, w_gate, (), [], BF16, tm, "proj_merge_gates")

        lam_init = 0.8 - 0.6 * math.exp(-0.3 * l)
        lq = lam_qk[l].astype(F32)
        lam = (jnp.exp(jnp.sum(lq[0] * lq[1])) - jnp.exp(jnp.sum(lq[2] * lq[3])) + lam_init).reshape(1)

        z_ab = z_ab.reshape(b, s, 2 * w_a)
        y_a = _diff_attention(lam, qk.reshape(b, s, 2 * qk_w), v_a.reshape(b, s, w_a), z_ab,
                              diff_norm_g[l], 1.0 - lam_init)
        y_b = _gdn(qkv_b.reshape(b, s, 3 * w_a), gcol.reshape(b, s, LANES),
                   grow.reshape(2 * N_HEADS, b, s // CHUNK, CHUNK), z_ab, conv_w[l], gdn_norm_g[l])

        final = l == depth - 1
        g_next = final_g if final else norm_g[l + 1]
        outs = _merge(y_a.reshape(m, w_a), y_b.reshape(m, w_a), merge_gates, x2,
                      w_branch[l, 0].astype(BF16), w_branch[l, 1].astype(BF16), w_out[l].astype(BF16),
                      g_next, final)
        if final:
            x2 = outs[0]
        else:
            x2, h = outs
    return x2.reshape(b, s, d)


def kernel(x_prompt, x_sample, norm_g, w_in, conv_w, lam_qk, diff_norm_g, a_log, dt_bias, gdn_norm_g, w_branch, w_out, final_g):
    params = (norm_g, w_in, conv_w, lam_qk, diff_norm_g, a_log, dt_bias, gdn_norm_g, w_branch, w_out, final_g)
    return (_trunk(x_prompt, *params), _trunk(x_sample, *params))
```

```python
import functools
import math

import jax
import jax.numpy as jnp
from jax import lax
from jax.experimental import pallas as pl
from jax.experimental.pallas import tpu as pltpu

F32 = jnp.float32
BF16 = jnp.bfloat16

LANES = 128
N_HEADS = 8
HD_QK = 64
HEAD_W = 128
CONV_W = 4
CHUNK = 128
ROPE_THETA = 10000.0
NORM_EPS = 1e-6
SUBLN_EPS = 1e-5
VMEM_LIMIT = 56 * 1024 * 1024
MASKED = -1e30


def _cparams(sem):
    return pltpu.CompilerParams(dimension_semantics=sem, vmem_limit_bytes=VMEM_LIMIT)


def _sigmoid(x):
    return 0.5 * jnp.tanh(0.5 * x) + 0.5


def _silu(x):
    h = 0.5 * x
    return h * jnp.tanh(h) + h


def _rmsnorm_kernel(x_ref, g_ref, o_ref, *, eps):
    x = x_ref[...]
    y = x * lax.rsqrt(jnp.mean(x * x, axis=-1, keepdims=True) + eps)
    o_ref[...] = (y * g_ref[...]).astype(o_ref.dtype)


def _rmsnorm(x, g, out_dtype, tm=1024):
    m, d = x.shape
    tm = min(tm, m)
    return pl.pallas_call(
        functools.partial(_rmsnorm_kernel, eps=NORM_EPS),
        grid=(m // tm,),
        in_specs=[pl.BlockSpec((tm, d), lambda i: (i, 0)), pl.BlockSpec((1, d), lambda i: (0, 0))],
        out_specs=pl.BlockSpec((tm, d), lambda i: (i, 0)),
        out_shape=jax.ShapeDtypeStruct((m, d), out_dtype),
        compiler_params=_cparams(("parallel",)),
        name="rmsnorm",
    )(x, g.reshape(1, d))


PROJ_COLS = 512


def _proj_plain_kernel(h_ref, w_ref, o_ref):
    h = h_ref[...]
    for j in range(o_ref.shape[1] // PROJ_COLS):
        cols = slice(j * PROJ_COLS, (j + 1) * PROJ_COLS)
        o_ref[:, cols] = jnp.dot(h, w_ref[:, cols], preferred_element_type=F32).astype(o_ref.dtype)


def _proj_sigmoid_kernel(h_ref, w_ref, o_ref):
    h = h_ref[...]
    for j in range(o_ref.shape[1] // PROJ_COLS):
        cols = slice(j * PROJ_COLS, (j + 1) * PROJ_COLS)
        acc = jnp.dot(h, w_ref[:, cols], preferred_element_type=F32)
        o_ref[:, cols] = _sigmoid(acc).astype(o_ref.dtype)


def _proj_rope_kernel(h_ref, w_ref, cos_ref, sin_ref, o_ref):
    h = h_ref[...]
    cos = cos_ref[...]
    sin = sin_ref[...]
    lane = lax.broadcasted_iota(jnp.int32, cos.shape, 1)
    first_half = (lane % HD_QK) < (HD_QK // 2)
    for j in range(o_ref.shape[1] // PROJ_COLS):
        acc = jnp.dot(h, w_ref[:, j * PROJ_COLS:(j + 1) * PROJ_COLS], preferred_element_type=F32)
        for g in range(PROJ_COLS // LANES):
            a = acc[:, g * LANES:(g + 1) * LANES]
            rot = jnp.where(first_half, pltpu.roll(a, LANES - HD_QK // 2, 1), pltpu.roll(a, HD_QK // 2, 1))
            lanes = slice(j * PROJ_COLS + g * LANES, j * PROJ_COLS + (g + 1) * LANES)
            o_ref[:, lanes] = (a * cos + rot * sin).astype(o_ref.dtype)


def _proj(kernel_fn, h, w, extras, extra_specs, out_dtype, tm, name):
    m, d = h.shape
    n = w.shape[1]
    return pl.pallas_call(
        kernel_fn,
        grid=(m // tm,),
        in_specs=[pl.BlockSpec((tm, d), lambda i: (i, 0)), pl.BlockSpec((d, n), lambda i: (0, 0))] + extra_specs,
        out_specs=pl.BlockSpec((tm, n), lambda i: (i, 0)),
        out_shape=jax.ShapeDtypeStruct((m, n), out_dtype),
        compiler_params=_cparams(("parallel",)),
        name=name,
    )(h, w, *extras)


def _proj_gdn_gate_kernel(h_ref, w_ref, alog_ref, dtb_ref, col_ref, row_ref):
    acc = jnp.dot(h_ref[...], w_ref[...], preferred_element_type=F32)
    lane = lax.broadcasted_iota(jnp.int32, acc.shape, 1)
    x = acc + dtb_ref[...]
    softplus = jnp.maximum(x, 0.0) + jnp.log(1.0 + jnp.exp(-jnp.abs(x)))
    g = jnp.where(lane < 2 * N_HEADS, (-math.log2(math.e)) * jnp.exp(alog_ref[...]) * softplus, 0.0)
    beta = _sigmoid(acc)
    r = lax.broadcasted_iota(jnp.int32, (CHUNK, CHUNK), 0)
    c = lax.broadcasted_iota(jnp.int32, (CHUNK, CHUNK), 1)
    tril = jnp.where(r >= c, 1.0, 0.0).astype(F32)
    lane_c = lax.broadcasted_iota(jnp.int32, (CHUNK, LANES), 1)
    for ci in range(acc.shape[0] // CHUNK):
        rows = slice(ci * CHUNK, (ci + 1) * CHUNK)
        g_c = g[rows]
        prefix = jnp.dot(tril, g_c, preferred_element_type=F32, precision=lax.Precision.HIGHEST)
        suffix = prefix[CHUNK - 1:CHUNK, :] - prefix + g_c
        out = jnp.where(lane_c < N_HEADS, prefix, jnp.where(lane_c < 2 * N_HEADS, suffix, beta[rows]))
        col_ref[rows, :] = out
        row_ref[:, rows] = out.T[:2 * N_HEADS, :]


def _proj_gdn_gates(h, w, alog_vec, dtb_vec, tm):
    m, d = h.shape
    vec = pl.BlockSpec((1, LANES), lambda i: (0, 0))
    return pl.pallas_call(
        _proj_gdn_gate_kernel,
        grid=(m // tm,),
        in_specs=[pl.BlockSpec((tm, d), lambda i: (i, 0)), pl.BlockSpec((d, LANES), lambda i: (0, 0)), vec, vec],
        out_specs=[pl.BlockSpec((tm, LANES), lambda i: (i, 0)), pl.BlockSpec((2 * N_HEADS, tm), lambda i: (0, i))],
        out_shape=[jax.ShapeDtypeStruct((m, LANES), F32), jax.ShapeDtypeStruct((2 * N_HEADS, m), F32)],
        compiler_params=_cparams(("parallel",)),
        name="proj_delta_gates",
    )(h, w, alog_vec, dtb_vec)


ATTN_ROW_BLOCKS = 4


def _attn_kernel(lam_ref, q_ref, k_ref, v_ref, z_ref, g_ref, o_ref, s0_ref, s1_ref, acc_ref, m_ref, *, tk, post_scale):
    tq = q_ref.shape[1]
    s_len = k_ref.shape[1]
    nk = s_len // tk
    nrb = ATTN_ROW_BLOCKS
    rb = 2 * tq // nrb
    q = q_ref[0]
    lane = lax.broadcasted_iota(jnp.int32, q.shape, 1)
    zero = jnp.zeros_like(q)
    q2 = jnp.concatenate([jnp.where(lane < HD_QK, q, zero), jnp.where(lane >= HD_QK, q, zero)], axis=0)
    ones = jnp.ones((tk, HEAD_W), BF16)

    def scores(kt, s_ref, r):
        rows = slice(r * rb, (r + 1) * rb)
        k = k_ref[0, pl.ds(pl.multiple_of(kt * tk, tk), tk), :]
        s_ref[rows, :] = lax.dot_general(q2[rows], k, (((1,), (1,)), ((), ())), preferred_element_type=F32)

    def update(kt, s_ref, r):
        rows = slice(r * rb, (r + 1) * rb)
        s = s_ref[rows, :]
        m_prev = m_ref[rows, :]
        m_new = jnp.maximum(m_prev, jnp.max(s, axis=1, keepdims=True))
        alpha = jnp.exp2(m_prev - m_new)
        p = jnp.exp2(s - m_new).astype(BF16)
        v_aug = jnp.concatenate([v_ref[0, pl.ds(pl.multiple_of(kt * tk, tk), tk), :], ones], axis=1)
        acc_ref[rows, :] = alpha * acc_ref[rows, :] + jnp.dot(p, v_aug, preferred_element_type=F32)
        m_ref[rows, :] = m_new

    def step(kt_next, s_next, kt, s_cur):
        for r in range(nrb):
            if kt_next is not None:
                scores(kt_next, s_next, r)
            update(kt, s_cur, r)

    m_ref[...] = jnp.full(m_ref.shape, -jnp.inf, F32)
    acc_ref[...] = jnp.zeros(acc_ref.shape, F32)
    for r in range(nrb):
        scores(0, s0_ref, r)

    def body(j, _):
        kt = 2 * j
        step(kt + 1, s1_ref, kt, s0_ref)
        step(kt + 2, s0_ref, kt + 1, s1_ref)
        return 0

    lax.fori_loop(0, nk // 2 - 1, body, 0)
    step(nk - 1, s1_ref, nk - 2, s0_ref)
    step(None, None, nk - 1, s1_ref)
    acc = acc_ref[...]
    o2 = acc[:, :HEAD_W] / acc[:, HEAD_W:]
    o = o2[:tq] - lam_ref[0] * o2[tq:]
    y = o * lax.rsqrt(jnp.mean(o * o, axis=-1, keepdims=True) + SUBLN_EPS) * g_ref[...] * post_scale
    o_ref[0] = (y * _silu(z_ref[0].astype(F32))).astype(o_ref.dtype)


def _diff_attention(lam, qk, v, z, gain, post_scale, tq=1024, tk=2048):
    b, s, w = v.shape
    nh = w // HEAD_W
    tq = min(tq, s)
    tk = min(tk, s // 2)
    assert s % (2 * tk) == 0 and s % tq == 0
    return pl.pallas_call(
        functools.partial(_attn_kernel, tk=tk, post_scale=post_scale),
        scratch_shapes=[
            pltpu.VMEM((2 * tq, tk), F32),
            pltpu.VMEM((2 * tq, tk), F32),
            pltpu.VMEM((2 * tq, 2 * HEAD_W), F32),
            pltpu.VMEM((2 * tq, 1), F32),
        ],
        grid=(b, nh, s // tq),
        in_specs=[
            pl.BlockSpec(memory_space=pltpu.SMEM),
            pl.BlockSpec((1, tq, HEAD_W), lambda bi, hi, qi: (bi, qi, hi)),
            pl.BlockSpec((1, s, HEAD_W), lambda bi, hi, qi: (bi, 0, nh + hi)),
            pl.BlockSpec((1, s, HEAD_W), lambda bi, hi, qi: (bi, 0, hi)),
            pl.BlockSpec((1, tq, HEAD_W), lambda bi, hi, qi: (bi, qi, hi)),
            pl.BlockSpec((1, HEAD_W), lambda bi, hi, qi: (0, 0)),
        ],
        out_specs=pl.BlockSpec((1, tq, HEAD_W), lambda bi, hi, qi: (bi, qi, hi)),
        out_shape=jax.ShapeDtypeStruct((b, s, w), BF16),
        compiler_params=_cparams(("parallel", "parallel", "arbitrary")),
        name="diff_attention",
    )(lam, qk, qk, v, z, gain.reshape(1, HEAD_W))


def _dot(a, b):
    return jnp.dot(a.astype(BF16), b.astype(BF16), preferred_element_type=F32)


def _dot_nt(a, b):
    return lax.dot_general(a.astype(BF16), b.astype(BF16), (((1,), (1,)), ((), ())), preferred_element_type=F32)


INV_BLOCK = 64
MAX_LOCAL_UNROLL = 16
STATE_UNROLL = 16


def _unit_triangular_inverses(mats):
    n = mats[0].shape[0]
    row = lax.broadcasted_iota(jnp.int32, (n, n), 0)
    col = lax.broadcasted_iota(jnp.int32, (n, n), 1)
    same_block = (row // INV_BLOCK) == (col // INV_BLOCK)
    eye = jnp.where(row == col, 1.0, 0.0).astype(F32)
    a_diag = [jnp.where(same_block, a, 0.0) for a in mats]
    a_off = [jnp.where(same_block, 0.0, a) for a in mats]
    def mm(a16, b16):
        return jnp.dot(a16, b16, preferred_element_type=F32)

    ts = [eye - a for a in a_diag]
    a16 = [a.astype(BF16) for a in a_diag]
    qs = [mm(a, a) for a in a16]
    levels = int(math.log2(INV_BLOCK)) - 1
    for level in range(levels):
        q16 = [q.astype(BF16) for q in qs]
        t16 = [t.astype(BF16) for t in ts]
        if level < levels - 1:
            prods = [mm(q, jnp.concatenate([q, t], axis=1)) for q, t in zip(q16, t16)]
            qs = [p[:, :n] for p in prods]
            ts = [t + p[:, n:] for t, p in zip(ts, prods)]
        else:
            prods = [mm(q, t) for q, t in zip(q16, t16)]
            ts = [t + p for t, p in zip(ts, prods)]
    t16 = [t.astype(BF16) for t in ts]
    offs = [mm(a.astype(BF16), t) for a, t in zip(a_off, t16)]
    corr = [mm(t, o.astype(BF16)) for t, o in zip(t16, offs)]
    return [t - c for t, c in zip(ts, corr)]


def _conv_silu(x_ref, w_ref, blk, rows, n_blk):
    halo = 16
    t0 = pl.multiple_of(blk * rows, rows)
    xc = x_ref[0, pl.ds(t0, rows), :].astype(F32)
    prev_start = pl.multiple_of(jnp.maximum(t0 - halo, 0), halo)
    next_start = pl.multiple_of(jnp.minimum(t0 + rows, n_blk * rows - halo), halo)
    prev = jnp.where(blk == 0, 0.0, x_ref[0, pl.ds(prev_start, halo), :].astype(F32))
    nxt = jnp.where(blk == n_blk - 1, 0.0, x_ref[0, pl.ds(next_start, halo), :].astype(F32))
    xx = jnp.concatenate([prev, xc, nxt], axis=0)
    n = rows + 2 * halo
    x_m1 = pltpu.roll(xx, 1, 0)[halo:halo + rows]
    x_p1 = pltpu.roll(xx, n - 1, 0)[halo:halo + rows]
    x_p2 = pltpu.roll(xx, n - 2, 0)[halo:halo + rows]
    w = w_ref[...]
    y = x_m1 * w[0:1, :] + xc * w[1:2, :] + x_p1 * w[2:3, :] + x_p2 * w[3:4, :]
    return _silu(y)


def _gdn_kernel(q_ref, k_ref, v_ref, gcol_ref, grow_ref, z_ref, wq_ref, wk_ref, wv_ref, gain_ref, o_ref,
                qkv0_ref, qkv1_ref, u_ref, wqg_ref, sm_ref, sn_ref, attn_ref, egl_ref, dir_ref, *, unroll):
    s_len = q_ref.shape[1]
    n_chunks = s_len // CHUNK
    head = pl.program_id(1)
    row = lax.broadcasted_iota(jnp.int32, (CHUNK, CHUNK), 0)
    col = lax.broadcasted_iota(jnp.int32, (CHUNK, CHUNK), 1)
    lane = lax.broadcasted_iota(jnp.int32, (CHUNK, LANES), 1)

    def conv_group(j, qkv_ref):
        for g in range(unroll):
            n = j * unroll + g
            rows = pl.ds(g * CHUNK, CHUNK)
            q = _conv_silu(q_ref, wq_ref, n, CHUNK, n_chunks)
            k = _conv_silu(k_ref, wk_ref, n, CHUNK, n_chunks)
            qkv_ref[0, rows, :] = q * (lax.rsqrt(jnp.sum(q * q, axis=-1, keepdims=True) + NORM_EPS) * (HEAD_W ** -0.5))
            qkv_ref[1, rows, :] = k * lax.rsqrt(jnp.sum(k * k, axis=-1, keepdims=True) + NORM_EPS)
            qkv_ref[2, rows, :] = _conv_silu(v_ref, wv_ref, n, CHUNK, n_chunks)

    def local_group(j, qkv_ref):
        ns = [j * unroll + g for g in range(unroll)]
        t0s = [pl.multiple_of(n * CHUNK, CHUNK) for n in ns]
        load = lambda i, g: qkv_ref[i, pl.ds(g * CHUNK, CHUNK), :]
        qk2s = [_dot_nt(jnp.concatenate([load(0, g), load(1, g)], axis=0), load(1, g))
                for g in range(unroll)]
        chains = []
        for g, n in enumerate(ns):
            gates = gcol_ref[0, pl.ds(t0s[g], CHUNK), :]
            for d in range(2):
                sel = head + d * N_HEADS
                gc = jnp.sum(jnp.where(lane == sel, gates, 0.0), axis=1, keepdims=True)
                beta = jnp.sum(jnp.where(lane == sel + 2 * N_HEADS, gates, 0.0), axis=1, keepdims=True)
                gc_row = grow_ref[sel, 0, pl.ds(n, 1), :]
                incl, strict, last = (row >= col, row > col, CHUNK - 1) if d == 0 else (row <= col, row < col, 0)
                decay = jnp.exp2(jnp.where(incl, gc - gc_row, MASKED))
                a = jnp.where(strict, qk2s[g][CHUNK:] * beta * decay, 0.0)
                attn_ref[d, pl.ds(t0s[g], CHUNK), :] = (qk2s[g][:CHUNK] * decay).astype(BF16)
                chains.append((g, d, gc, beta, gc_row[:, last:last + 1], a))
        ts = _unit_triangular_inverses([c[-1] for c in chains])
        egcs = [jnp.exp2(c[2]) for c in chains]
        uws = [_dot(t, jnp.concatenate([load(2, c[0]) * c[3], load(1, c[0]) * (c[3] * egc)], axis=1))
               for t, c, egc in zip(ts, chains, egcs)]
        kdts = [(load(1, c[0]) * jnp.exp2(c[4] - c[2])).T for c in chains]
        trans = [_dot(kdt, uw) for kdt, uw in zip(kdts, uws)]
        for (g, d, gc, beta, gl, _), egc, uw, tr in zip(chains, egcs, uws, trans):
            t0, n = t0s[g], ns[g]
            sn_ref[d, pl.ds(t0, CHUNK), :] = tr[:, :HEAD_W]
            sm_ref[d, pl.ds(t0, CHUNK), :] = tr[:, HEAD_W:].astype(BF16)
            u_ref[d, pl.ds(t0, CHUNK), :] = uw[:, :HEAD_W]
            wqg_ref[d, pl.ds(pl.multiple_of(2 * t0, CHUNK), CHUNK), :] = uw[:, HEAD_W:].astype(BF16)
            wqg_ref[d, pl.ds(pl.multiple_of(2 * t0 + CHUNK, CHUNK), CHUNK), :] = (load(0, g) * egc).astype(BF16)
            egl_ref[d, pl.ds(n, 1), :] = jnp.broadcast_to(jnp.exp2(gl), (1, LANES))

    n_groups = n_chunks // unroll
    conv_group(0, qkv0_ref)

    def local_body(i, _):
        j = 2 * i
        conv_group(j + 1, qkv1_ref)
        local_group(j, qkv0_ref)
        conv_group(jnp.minimum(j + 2, n_groups - 1), qkv0_ref)
        local_group(j + 1, qkv1_ref)
        return 0

    lax.fori_loop(0, n_groups // 2, local_body, 0)

    def chunk_starts(i):
        return [pl.multiple_of(n * CHUNK, CHUNK) for n in (i, n_chunks - 1 - i)]

    def emit_outputs(i, v_new, rq):
        t0s = chunk_starts(i)
        intra = [jnp.dot(attn_ref[d, pl.ds(t0s[d], CHUNK), :], v_new[d], preferred_element_type=F32) for d in range(2)]
        for d in range(2):
            dir_ref[d, pl.ds(t0s[d], CHUNK), :] = rq[d] + intra[d]

    def state_step(i, states, pending):
        t0s = chunk_starts(i)
        s16 = [states[d].astype(BF16) for d in range(2)]
        ms = [jnp.dot(sm_ref[d, pl.ds(t0s[d], CHUNK), :], s16[d], preferred_element_type=F32) for d in range(2)]
        rs = [jnp.dot(wqg_ref[d, pl.ds(pl.multiple_of(2 * t0s[d], 2 * CHUNK), 2 * CHUNK), :],
                      s16[d], preferred_element_type=F32) for d in range(2)]
        if pending is not None:
            emit_outputs(*pending)
        new_states = tuple(states[d] * egl_ref[d, pl.ds(n, 1), :] - ms[d] + sn_ref[d, pl.ds(t0s[d], CHUNK), :]
                           for d, n in enumerate((i, n_chunks - 1 - i)))
        v_new = tuple((u_ref[d, pl.ds(t0s[d], CHUNK), :] - rs[d][:CHUNK]).astype(BF16) for d in range(2))
        rq = tuple(rs[d][CHUNK:] for d in range(2))
        return new_states, (i, v_new, rq)

    def state_steps(j, states):
        pending = None
        for g in range(STATE_UNROLL):
            states, pending = state_step(STATE_UNROLL * j + g, states, pending)
        emit_outputs(*pending)
        return states

    zero = jnp.zeros((HEAD_W, HEAD_W), F32)
    lax.fori_loop(0, n_chunks // STATE_UNROLL, state_steps, (zero, zero))

    def finish_body(j, _):
        t0 = pl.multiple_of(j * (unroll * CHUNK), unroll * CHUNK)
        rows = pl.ds(t0, unroll * CHUNK)
        o = dir_ref[0, rows, :] + dir_ref[1, rows, :]
        y = o * lax.rsqrt(jnp.mean(o * o, axis=-1, keepdims=True) + NORM_EPS) * gain_ref[...]
        o_ref[0, rows, :] = (y * _silu(z_ref[0, rows, :].astype(F32))).astype(o_ref.dtype)
        return 0

    lax.fori_loop(0, n_groups, finish_body, 0)


def _gdn(qkv, gcol, grow, z, conv_w, gain):
    b, s, _ = qkv.shape
    nh = N_HEADS
    n_chunks = s // CHUNK
    unroll = min(MAX_LOCAL_UNROLL, n_chunks // 2)
    assert s % CHUNK == 0 and n_chunks % (2 * unroll) == 0 and n_chunks % STATE_UNROLL == 0
    seq_spec = lambda off: pl.BlockSpec((1, s, HEAD_W), lambda bi, hi: (bi, 0, hi + off))
    w_spec = lambda off: pl.BlockSpec((CONV_W, HEAD_W), lambda bi, hi: (0, hi + off))
    return pl.pallas_call(
        functools.partial(_gdn_kernel, unroll=unroll),
        grid=(b, nh),
        in_specs=[
            seq_spec(0), seq_spec(nh), seq_spec(2 * nh),
            pl.BlockSpec((1, s, LANES), lambda bi, hi: (bi, 0, 0)),
            pl.BlockSpec((2 * nh, 1, n_chunks, CHUNK), lambda bi, hi: (0, bi, 0, 0)),
            seq_spec(nh),
            w_spec(0), w_spec(nh), w_spec(2 * nh),
            pl.BlockSpec((1, HEAD_W), lambda bi, hi: (0, 0)),
        ],
        out_specs=seq_spec(0),
        out_shape=jax.ShapeDtypeStruct((b, s, nh * HEAD_W), BF16),
        scratch_shapes=[
            pltpu.VMEM((3, unroll * CHUNK, HEAD_W), F32),
            pltpu.VMEM((3, unroll * CHUNK, HEAD_W), F32),
            pltpu.VMEM((2, s, HEAD_W), F32),
            pltpu.VMEM((2, 2 * s, HEAD_W), BF16),
            pltpu.VMEM((2, s, HEAD_W), BF16),
            pltpu.VMEM((2, s, HEAD_W), F32),
            pltpu.VMEM((2, s, CHUNK), BF16),
            pltpu.VMEM((2, n_chunks, LANES), F32),
            pltpu.VMEM((2, s, HEAD_W), F32),
        ],
        compiler_params=_cparams(("parallel", "arbitrary")),
        name="gated_delta",
    )(qkv, qkv, qkv, gcol, grow, z, conv_w, conv_w, conv_w, gain.reshape(1, HEAD_W))


def _merge_kernel(ya_ref, yb_ref, gate_ref, x_ref, wa_ref, wb_ref, wo_ref, g_ref, *out_refs, final, tn):
    d = x_ref.shape[1]
    merged = []
    for j in range(d // tn):
        sl = slice(j * tn, (j + 1) * tn)
        pa = jnp.dot(ya_ref[...], wa_ref[:, sl], preferred_element_type=F32)
        pb = jnp.dot(yb_ref[...], wb_ref[:, sl], preferred_element_type=F32)
        ga = gate_ref[:, j * tn:(j + 1) * tn].astype(F32)
        gb = gate_ref[:, d + j * tn:d + (j + 1) * tn].astype(F32)
        merged.append((ga * pa + gb * pb).astype(BF16))
    merged = jnp.concatenate(merged, axis=1)
    x = x_ref[...] + jnp.dot(merged, wo_ref[...], preferred_element_type=F32)
    y = x * lax.rsqrt(jnp.mean(x * x, axis=-1, keepdims=True) + NORM_EPS) * g_ref[...]
    if final:
        out_refs[0][...] = y
    else:
        out_refs[0][...] = x
        out_refs[1][...] = y.astype(BF16)


def _merge(ya, yb, gates, x, wa, wb, wo, g_next, final, tm=512, tn=256):
    m, d = x.shape
    tm = min(tm, m)
    row = lambda width: pl.BlockSpec((tm, width), lambda i: (i, 0))
    full = lambda r, c: pl.BlockSpec((r, c), lambda i: (0, 0))
    if final:
        out_shape = [jax.ShapeDtypeStruct((m, d), F32)]
        out_specs = [row(d)]
    else:
        out_shape = [jax.ShapeDtypeStruct((m, d), F32), jax.ShapeDtypeStruct((m, d), BF16)]
        out_specs = [row(d), row(d)]
    return pl.pallas_call(
        functools.partial(_merge_kernel, final=final, tn=tn),
        grid=(m // tm,),
        in_specs=[row(d), row(d), row(2 * d), row(d), full(d, d), full(d, d), full(d, d), full(1, d)],
        out_specs=out_specs,
        out_shape=out_shape,
        compiler_params=_cparams(("parallel",)),
        name="merge_out",
    )(ya, yb, gates, x, wa, wb, wo, g_next.reshape(1, d))


def _rope_tables(s):
    inv = ROPE_THETA ** (-jnp.arange(0, HD_QK, 2, dtype=F32) / HD_QK)
    ang = jnp.arange(s, dtype=F32)[:, None] * inv[None, :]
    ang = jnp.concatenate([ang, ang], -1)
    sign = jnp.where(jnp.arange(HD_QK) < HD_QK // 2, -1.0, 1.0).astype(F32)
    cos = jnp.tile(jnp.cos(ang), (1, LANES // HD_QK))
    sin = jnp.tile(jnp.sin(ang) * sign[None, :], (1, LANES // HD_QK))
    return cos, sin


def _trunk(x, norm_g, w_in, conv_w, lam_qk, diff_norm_g, a_log, dt_bias, gdn_norm_g, w_branch, w_out, final_g):
    b, s, d = x.shape
    depth = w_in.shape[0]
    m = b * s
    w_a = N_HEADS * HEAD_W
    qk_w = N_HEADS * 2 * HD_QK
    c_q, c_k, c_v, c_za = 0, qk_w, 2 * qk_w, 2 * qk_w + w_a
    c_qkvb = c_za + w_a
    c_zb = c_qkvb + 3 * w_a
    c_a = c_zb + w_a
    c_b = c_a + 2 * N_HEADS
    c_gate = c_b + 2 * N_HEADS

    cos, sin = _rope_tables(s)
    tm = min(1024, s)
    pos_blocks = s // tm
    rope_specs = [pl.BlockSpec((tm, LANES), lambda i: (i % pos_blocks, 0))] * 2

    x2 = x.reshape(m, d)
    h = _rmsnorm(x2, norm_g[0], BF16)
    for l in range(depth):
        w = w_in[l]
        scale = HD_QK ** -0.5 * math.log2(math.e)
        w_qk = jnp.concatenate([w[:, c_q:c_k] * scale, w[:, c_k:c_v]], axis=1).astype(BF16)
        w_v = w[:, c_v:c_za].astype(BF16)
        w_z = jnp.concatenate([w[:, c_za:c_qkvb], w[:, c_zb:c_a]], axis=1).astype(BF16)
        w_qkvb = w[:, c_qkvb:c_zb].astype(BF16)
        w_ab = jnp.pad(w[:, c_a:c_gate], ((0, 0), (0, LANES - 4 * N_HEADS))).astype(BF16)
        w_gate = w[:, c_gate:].astype(BF16)
        alog_vec = jnp.pad(a_log[l].reshape(1, -1), ((0, 0), (0, LANES - 2 * N_HEADS)))
        dtb_vec = jnp.pad(dt_bias[l].reshape(1, -1), ((0, 0), (0, LANES - 2 * N_HEADS)))

        qk = _proj(_proj_rope_kernel, h, w_qk, (cos, sin), rope_specs, BF16, tm, "proj_qk_rope")
        v_a = _proj(_proj_plain_kernel, h, w_v, (), [], BF16, tm, "proj_v")
        z_ab = _proj(_proj_plain_kernel, h, w_z, (), [], BF16, tm, "proj_z")
        qkv_b = _proj(_proj_plain_kernel, h, w_qkvb, (), [], BF16, tm, "proj_qkv_delta")
        gcol, grow = _proj_gdn_gates(h, w_ab, alog_vec, dtb_vec, tm)
        merge_gates = _proj(_proj_sigmoid_kernel, h, w_gate, (), [], BF16, tm, "proj_merge_gates")

        lam_init = 0.8 - 0.6 * math.exp(-0.3 * l)
        lq = lam_qk[l].astype(F32)
        lam = (jnp.exp(jnp.sum(lq[0] * lq[1])) - jnp.exp(jnp.sum(lq[2] * lq[3])) + lam_init).reshape(1)

        z_ab = z_ab.reshape(b, s, 2 * w_a)
        y_a = _diff_attention(lam, qk.reshape(b, s, 2 * qk_w), v_a.reshape(b, s, w_a), z_ab,
                              diff_norm_g[l], 1.0 - lam_init)
        y_b = _gdn(qkv_b.reshape(b, s, 3 * w_a), gcol.reshape(b, s, LANES),
                   grow.reshape(2 * N_HEADS, b, s // CHUNK, CHUNK), z_ab, conv_w[l], gdn_norm_g[l])

        final = l == depth - 1
        g_next = final_g if final else norm_g[l + 1]
        outs = _merge(y_a.reshape(m, w_a), y_b.reshape(m, w_a), merge_gates, x2,
                      w_branch[l, 0].astype(BF16), w_branch[l, 1].astype(BF16), w_out[l].astype(BF16),
                      g_next, final)
        if final:
            x2 = outs[0]
        else:
            x2, h = outs
    return x2.reshape(b, s, d)


def kernel(x_prompt, x_sample, norm_g, w_in, conv_w, lam_qk, diff_norm_g, a_log, dt_bias, gdn_norm_g, w_branch, w_out, final_g):
    params = (norm_g, w_in, conv_w, lam_qk, diff_norm_g, a_log, dt_bias, gdn_norm_g, w_branch, w_out, final_g)
    return (_trunk(x_prompt, *params), _trunk(x_sample, *params))
```

```python
import functools
import math

import jax
import jax.numpy as jnp
from jax import lax
from jax.experimental import pallas as pl
from jax.experimental.pallas import tpu as pltpu

F32 = jnp.float32
BF16 = jnp.bfloat16

LANES = 128
N_HEADS = 8
HD_QK = 64
HEAD_W = 128
CONV_W = 4
CHUNK = 128
ROPE_THETA = 10000.0
NORM_EPS = 1e-6
SUBLN_EPS = 1e-5
VMEM_LIMIT = 56 * 1024 * 1024
MASKED = -1e30


def _cparams(sem):
    return pltpu.CompilerParams(dimension_semantics=sem, vmem_limit_bytes=VMEM_LIMIT)


def _sigmoid(x):
    return 0.5 * jnp.tanh(0.5 * x) + 0.5


def _silu(x):
    h = 0.5 * x
    return h * jnp.tanh(h) + h


def _rmsnorm_kernel(x_ref, g_ref, o_ref, *, eps):
    x = x_ref[...]
    y = x * lax.rsqrt(jnp.mean(x * x, axis=-1, keepdims=True) + eps)
    o_ref[...] = (y * g_ref[...]).astype(o_ref.dtype)


def _rmsnorm(x, g, out_dtype, tm=1024):
    m, d = x.shape
    tm = min(tm, m)
    return pl.pallas_call(
        functools.partial(_rmsnorm_kernel, eps=NORM_EPS),
        grid=(m // tm,),
        in_specs=[pl.BlockSpec((tm, d), lambda i: (i, 0)), pl.BlockSpec((1, d), lambda i: (0, 0))],
        out_specs=pl.BlockSpec((tm, d), lambda i: (i, 0)),
        out_shape=jax.ShapeDtypeStruct((m, d), out_dtype),
        compiler_params=_cparams(("parallel",)),
        name="rmsnorm",
    )(x, g.reshape(1, d))


PROJ_COLS = 512


def _proj_plain_kernel(h_ref, w_ref, o_ref):
    h = h_ref[...]
    for j in range(o_ref.shape[1] // PROJ_COLS):
        cols = slice(j * PROJ_COLS, (j + 1) * PROJ_COLS)
        o_ref[:, cols] = jnp.dot(h, w_ref[:, cols], preferred_element_type=F32).astype(o_ref.dtype)


def _proj_sigmoid_kernel(h_ref, w_ref, o_ref):
    h = h_ref[...]
    for j in range(o_ref.shape[1] // PROJ_COLS):
        cols = slice(j * PROJ_COLS, (j + 1) * PROJ_COLS)
        acc = jnp.dot(h, w_ref[:, cols], preferred_element_type=F32)
        o_ref[:, cols] = _sigmoid(acc).astype(o_ref.dtype)


def _proj_rope_kernel(h_ref, w_ref, cos_ref, sin_ref, o_ref):
    h = h_ref[...]
    cos = cos_ref[...]
    sin = sin_ref[...]
    lane = lax.broadcasted_iota(jnp.int32, cos.shape, 1)
    first_half = (lane % HD_QK) < (HD_QK // 2)
    for j in range(o_ref.shape[1] // PROJ_COLS):
        acc = jnp.dot(h, w_ref[:, j * PROJ_COLS:(j + 1) * PROJ_COLS], preferred_element_type=F32)
        for g in range(PROJ_COLS // LANES):
            a = acc[:, g * LANES:(g + 1) * LANES]
            rot = jnp.where(first_half, pltpu.roll(a, LANES - HD_QK // 2, 1), pltpu.roll(a, HD_QK // 2, 1))
            lanes = slice(j * PROJ_COLS + g * LANES, j * PROJ_COLS + (g + 1) * LANES)
            o_ref[:, lanes] = (a * cos + rot * sin).astype(o_ref.dtype)


def _proj(kernel_fn, h, w, extras, extra_specs, out_dtype, tm, name):
    m, d = h.shape
    n = w.shape[1]
    return pl.pallas_call(
        kernel_fn,
        grid=(m // tm,),
        in_specs=[pl.BlockSpec((tm, d), lambda i: (i, 0)), pl.BlockSpec((d, n), lambda i: (0, 0))] + extra_specs,
        out_specs=pl.BlockSpec((tm, n), lambda i: (i, 0)),
        out_shape=jax.ShapeDtypeStruct((m, n), out_dtype),
        compiler_params=_cparams(("parallel",)),
        name=name,
    )(h, w, *extras)


def _proj_gdn_gate_kernel(h_ref, w_ref, alog_ref, dtb_ref, col_ref, row_ref):
    acc = jnp.dot(h_ref[...], w_ref[...], preferred_element_type=F32)
    lane = lax.broadcasted_iota(jnp.int32, acc.shape, 1)
    x = acc + dtb_ref[...]
    softplus = jnp.maximum(x, 0.0) + jnp.log(1.0 + jnp.exp(-jnp.abs(x)))
    g = jnp.where(lane < 2 * N_HEADS, (-math.log2(math.e)) * jnp.exp(alog_ref[...]) * softplus, 0.0)
    beta = _sigmoid(acc)
    r = lax.broadcasted_iota(jnp.int32, (CHUNK, CHUNK), 0)
    c = lax.broadcasted_iota(jnp.int32, (CHUNK, CHUNK), 1)
    tril = jnp.where(r >= c, 1.0, 0.0).astype(F32)
    lane_c = lax.broadcasted_iota(jnp.int32, (CHUNK, LANES), 1)
    for ci in range(acc.shape[0] // CHUNK):
        rows = slice(ci * CHUNK, (ci + 1) * CHUNK)
        g_c = g[rows]
        prefix = jnp.dot(tril, g_c, preferred_element_type=F32, precision=lax.Precision.HIGHEST)
        suffix = prefix[CHUNK - 1:CHUNK, :] - prefix + g_c
        out = jnp.where(lane_c < N_HEADS, prefix, jnp.where(lane_c < 2 * N_HEADS, suffix, beta[rows]))
        col_ref[rows, :] = out
        row_ref[:, rows] = out.T[:2 * N_HEADS, :]


def _proj_gdn_gates(h, w, alog_vec, dtb_vec, tm):
    m, d = h.shape
    vec = pl.BlockSpec((1, LANES), lambda i: (0, 0))
    return pl.pallas_call(
        _proj_gdn_gate_kernel,
        grid=(m // tm,),
        in_specs=[pl.BlockSpec((tm, d), lambda i: (i, 0)), pl.BlockSpec((d, LANES), lambda i: (0, 0)), vec, vec],
        out_specs=[pl.BlockSpec((tm, LANES), lambda i: (i, 0)), pl.BlockSpec((2 * N_HEADS, tm), lambda i: (0, i))],
        out_shape=[jax.ShapeDtypeStruct((m, LANES), F32), jax.ShapeDtypeStruct((2 * N_HEADS, m), F32)],
        compiler_params=_cparams(("parallel",)),
        name="proj_delta_gates",
    )(h, w, alog_vec, dtb_vec)


ATTN_ROW_BLOCKS = 4


def _attn_kernel(lam_ref, q_ref, k_ref, v_ref, z_ref, g_ref, o_ref, s0_ref, s1_ref, acc_ref, m_ref, *, tk, post_scale):
    tq = q_ref.shape[1]
    s_len = k_ref.shape[1]
    nk = s_len // tk
    nrb = ATTN_ROW_BLOCKS
    rb = 2 * tq // nrb
    q = q_ref[0]
    lane = lax.broadcasted_iota(jnp.int32, q.shape, 1)
    zero = jnp.zeros_like(q)
    q2 = jnp.concatenate([jnp.where(lane < HD_QK, q, zero), jnp.where(lane >= HD_QK, q, zero)], axis=0)
    ones = jnp.ones((tk, HEAD_W), BF16)

    def scores(kt, s_ref, r):
        rows = slice(r * rb, (r + 1) * rb)
        k = k_ref[0, pl.ds(pl.multiple_of(kt * tk, tk), tk), :]
        s_ref[rows, :] = lax.dot_general(q2[rows], k, (((1,), (1,)), ((), ())), preferred_element_type=F32)

    def update(kt, s_ref, r):
        rows = slice(r * rb, (r + 1) * rb)
        s = s_ref[rows, :]
        m_prev = m_ref[rows, :]
        m_new = jnp.maximum(m_prev, jnp.max(s, axis=1, keepdims=True))
        alpha = jnp.exp2(m_prev - m_new)
        p = jnp.exp2(s - m_new).astype(BF16)
        v_aug = jnp.concatenate([v_ref[0, pl.ds(pl.multiple_of(kt * tk, tk), tk), :], ones], axis=1)
        acc_ref[rows, :] = alpha * acc_ref[rows, :] + jnp.dot(p, v_aug, preferred_element_type=F32)
        m_ref[rows, :] = m_new

    def step(kt_next, s_next, kt, s_cur):
        for r in range(nrb):
            if kt_next is not None:
                scores(kt_next, s_next, r)
            update(kt, s_cur, r)

    m_ref[...] = jnp.full(m_ref.shape, -jnp.inf, F32)
    acc_ref[...] = jnp.zeros(acc_ref.shape, F32)
    for r in range(nrb):
        scores(0, s0_ref, r)

    def body(j, _):
        kt = 2 * j
        step(kt + 1, s1_ref, kt, s0_ref)
        step(kt + 2, s0_ref, kt + 1, s1_ref)
        return 0

    lax.fori_loop(0, nk // 2 - 1, body, 0)
    step(nk - 1, s1_ref, nk - 2, s0_ref)
    step(None, None, nk - 1, s1_ref)
    acc = acc_ref[...]
    o2 = acc[:, :HEAD_W] / acc[:, HEAD_W:]
    o = o2[:tq] - lam_ref[0] * o2[tq:]
    y = o * lax.rsqrt(jnp.mean(o * o, axis=-1, keepdims=True) + SUBLN_EPS) * g_ref[...] * post_scale
    o_ref[0] = (y * _silu(z_ref[0].astype(F32))).astype(o_ref.dtype)


def _diff_attention(lam, qk, v, z, gain, post_scale, tq=1024, tk=2048):
    b, s, w = v.shape
    nh = w // HEAD_W
    tq = min(tq, s)
    tk = min(tk, s // 2)
    assert s % (2 * tk) == 0 and s % tq == 0
    return pl.pallas_call(
        functools.partial(_attn_kernel, tk=tk, post_scale=post_scale),
        scratch_shapes=[
            pltpu.VMEM((2 * tq, tk), F32),
            pltpu.VMEM((2 * tq, tk), F32),
            pltpu.VMEM((2 * tq, 2 * HEAD_W), F32),
            pltpu.VMEM((2 * tq, 1), F32),
        ],
        grid=(b, nh, s // tq),
        in_specs=[
            pl.BlockSpec(memory_space=pltpu.SMEM),
            pl.BlockSpec((1, tq, HEAD_W), lambda bi, hi, qi: (bi, qi, hi)),
            pl.BlockSpec((1, s, HEAD_W), lambda bi, hi, qi: (bi, 0, nh + hi)),
            pl.BlockSpec((1, s, HEAD_W), lambda bi, hi, qi: (bi, 0, hi)),
            pl.BlockSpec((1, tq, HEAD_W), lambda bi, hi, qi: (bi, qi, hi)),
            pl.BlockSpec((1, HEAD_W), lambda bi, hi, qi: (0, 0)),
        ],
        out_specs=pl.BlockSpec((1, tq, HEAD_W), lambda bi, hi, qi: (bi, qi, hi)),
        out_shape=jax.ShapeDtypeStruct((b, s, w), BF16),
        compiler_params=_cparams(("parallel", "parallel", "arbitrary")),
        name="diff_attention",
    )(lam, qk, qk, v, z, gain.reshape(1, HEAD_W))


def _dot(a, b):
    return jnp.dot(a.astype(BF16), b.astype(BF16), preferred_element_type=F32)


def _dot_nt(a, b):
    return lax.dot_general(a.astype(BF16), b.astype(BF16), (((1,), (1,)), ((), ())), preferred_element_type=F32)


INV_BLOCK = 64
MAX_LOCAL_UNROLL = 16


def _unit_triangular_inverses(mats):
    n = mats[0].shape[0]
    row = lax.broadcasted_iota(jnp.int32, (n, n), 0)
    col = lax.broadcasted_iota(jnp.int32, (n, n), 1)
    same_block = (row // INV_BLOCK) == (col // INV_BLOCK)
    eye = jnp.where(row == col, 1.0, 0.0).astype(F32)
    a_diag = [jnp.where(same_block, a, 0.0) for a in mats]
    a_off = [jnp.where(same_block, 0.0, a) for a in mats]
    def mm(a16, b16):
        return jnp.dot(a16, b16, preferred_element_type=F32)

    ts = [eye - a for a in a_diag]
    a16 = [a.astype(BF16) for a in a_diag]
    qs = [mm(a, a) for a in a16]
    levels = int(math.log2(INV_BLOCK)) - 1
    for level in range(levels):
        q16 = [q.astype(BF16) for q in qs]
        t16 = [t.astype(BF16) for t in ts]
        if level < levels - 1:
            prods = [mm(q, jnp.concatenate([q, t], axis=1)) for q, t in zip(q16, t16)]
            qs = [p[:, :n] for p in prods]
            ts = [t + p[:, n:] for t, p in zip(ts, prods)]
        else:
            prods = [mm(q, t) for q, t in zip(q16, t16)]
            ts = [t + p for t, p in zip(ts, prods)]
    t16 = [t.astype(BF16) for t in ts]
    offs = [mm(a.astype(BF16), t) for a, t in zip(a_off, t16)]
    corr = [mm(t, o.astype(BF16)) for t, o in zip(t16, offs)]
    return [t - c for t, c in zip(ts, corr)]


def _conv_silu(x_ref, w_ref, blk, rows, n_blk):
    halo = 16
    t0 = pl.multiple_of(blk * rows, rows)
    xc = x_ref[0, pl.ds(t0, rows), :].astype(F32)
    prev_start = pl.multiple_of(jnp.maximum(t0 - halo, 0), halo)
    next_start = pl.multiple_of(jnp.minimum(t0 + rows, n_blk * rows - halo), halo)
    prev = jnp.where(blk == 0, 0.0, x_ref[0, pl.ds(prev_start, halo), :].astype(F32))
    nxt = jnp.where(blk == n_blk - 1, 0.0, x_ref[0, pl.ds(next_start, halo), :].astype(F32))
    xx = jnp.concatenate([prev, xc, nxt], axis=0)
    n = rows + 2 * halo
    x_m1 = pltpu.roll(xx, 1, 0)[halo:halo + rows]
    x_p1 = pltpu.roll(xx, n - 1, 0)[halo:halo + rows]
    x_p2 = pltpu.roll(xx, n - 2, 0)[halo:halo + rows]
    w = w_ref[...]
    y = x_m1 * w[0:1, :] + xc * w[1:2, :] + x_p1 * w[2:3, :] + x_p2 * w[3:4, :]
    return _silu(y)


def _gdn_kernel(q_ref, k_ref, v_ref, gcol_ref, grow_ref, z_ref, wq_ref, wk_ref, wv_ref, gain_ref, o_ref,
                qkv0_ref, qkv1_ref, u_ref, wqg_ref, sm_ref, sn_ref, attn_ref, egl_ref, dir_ref, *, unroll):
    s_len = q_ref.shape[1]
    n_chunks = s_len // CHUNK
    head = pl.program_id(1)
    row = lax.broadcasted_iota(jnp.int32, (CHUNK, CHUNK), 0)
    col = lax.broadcasted_iota(jnp.int32, (CHUNK, CHUNK), 1)
    lane = lax.broadcasted_iota(jnp.int32, (CHUNK, LANES), 1)

    def conv_group(j, qkv_ref):
        for g in range(unroll):
            n = j * unroll + g
            rows = pl.ds(g * CHUNK, CHUNK)
            q = _conv_silu(q_ref, wq_ref, n, CHUNK, n_chunks)
            k = _conv_silu(k_ref, wk_ref, n, CHUNK, n_chunks)
            qkv_ref[0, rows, :] = q * (lax.rsqrt(jnp.sum(q * q, axis=-1, keepdims=True) + NORM_EPS) * (HEAD_W ** -0.5))
            qkv_ref[1, rows, :] = k * lax.rsqrt(jnp.sum(k * k, axis=-1, keepdims=True) + NORM_EPS)
            qkv_ref[2, rows, :] = _conv_silu(v_ref, wv_ref, n, CHUNK, n_chunks)

    def local_group(j, qkv_ref):
        ns = [j * unroll + g for g in range(unroll)]
        t0s = [pl.multiple_of(n * CHUNK, CHUNK) for n in ns]
        load = lambda i, g: qkv_ref[i, pl.ds(g * CHUNK, CHUNK), :]
        qk2s = [_dot_nt(jnp.concatenate([load(0, g), load(1, g)], axis=0), load(1, g))
                for g in range(unroll)]
        chains = []
        for g, n in enumerate(ns):
            gates = gcol_ref[0, pl.ds(t0s[g], CHUNK), :]
            for d in range(2):
                sel = head + d * N_HEADS
                gc = jnp.sum(jnp.where(lane == sel, gates, 0.0), axis=1, keepdims=True)
                beta = jnp.sum(jnp.where(lane == sel + 2 * N_HEADS, gates, 0.0), axis=1, keepdims=True)
                gc_row = grow_ref[sel, 0, pl.ds(n, 1), :]
                incl, strict, last = (row >= col, row > col, CHUNK - 1) if d == 0 else (row <= col, row < col, 0)
                decay = jnp.exp2(jnp.where(incl, gc - gc_row, MASKED))
                a = jnp.where(strict, qk2s[g][CHUNK:] * beta * decay, 0.0)
                attn_ref[d, pl.ds(t0s[g], CHUNK), :] = (qk2s[g][:CHUNK] * decay).astype(BF16)
                chains.append((g, d, gc, beta, gc_row[:, last:last + 1], a))
        ts = _unit_triangular_inverses([c[-1] for c in chains])
        egcs = [jnp.exp2(c[2]) for c in chains]
        uws = [_dot(t, jnp.concatenate([load(2, c[0]) * c[3], load(1, c[0]) * (c[3] * egc)], axis=1))
               for t, c, egc in zip(ts, chains, egcs)]
        kdts = [(load(1, c[0]) * jnp.exp2(c[4] - c[2])).T for c in chains]
        trans = [_dot(kdt, uw) for kdt, uw in zip(kdts, uws)]
        for (g, d, gc, beta, gl, _), egc, uw, tr in zip(chains, egcs, uws, trans):
            t0, n = t0s[g], ns[g]
            sn_ref[d, pl.ds(t0, CHUNK), :] = tr[:, :HEAD_W]
            sm_ref[d, pl.ds(t0, CHUNK), :] = tr[:, HEAD_W:].astype(BF16)
            u_ref[d, pl.ds(t0, CHUNK), :] = uw[:, :HEAD_W]
            wqg_ref[d, pl.ds(pl.multiple_of(2 * t0, CHUNK), CHUNK), :] = uw[:, HEAD_W:].astype(BF16)
            wqg_ref[d, pl.ds(pl.multiple_of(2 * t0 + CHUNK, CHUNK), CHUNK), :] = (load(0, g) * egc).astype(BF16)
            egl_ref[d, pl.ds(n, 1), :] = jnp.broadcast_to(jnp.exp2(gl), (1, LANES))

    n_groups = n_chunks // unroll
    conv_group(0, qkv0_ref)

    def local_body(i, _):
        j = 2 * i
        conv_group(j + 1, qkv1_ref)
        local_group(j, qkv0_ref)
        conv_group(jnp.minimum(j + 2, n_groups - 1), qkv0_ref)
        local_group(j + 1, qkv1_ref)
        return 0

    lax.fori_loop(0, n_groups // 2, local_body, 0)

    def chunk_starts(i):
        return [n * CHUNK for n in (i, n_chunks - 1 - i)]

    def emit_outputs(i, v_new, rq):
        t0s = chunk_starts(i)
        intra = [jnp.dot(attn_ref[d, pl.ds(t0s[d], CHUNK), :], v_new[d], preferred_element_type=F32) for d in range(2)]
        for d in range(2):
            rows = pl.ds(t0s[d], CHUNK)
            o = rq[d] + intra[d]
            if i < n_chunks // 2:
                dir_ref[d, rows, :] = o
            else:
                o = o + dir_ref[1 - d, rows, :]
                y = o * lax.rsqrt(jnp.mean(o * o, axis=-1, keepdims=True) + NORM_EPS) * gain_ref[...]
                o_ref[0, rows, :] = (y * _silu(z_ref[0, rows, :].astype(F32))).astype(o_ref.dtype)

    def state_step(i, states, pending):
        t0s = chunk_starts(i)
        s16 = [states[d].astype(BF16) for d in range(2)]
        ms = [jnp.dot(sm_ref[d, pl.ds(t0s[d], CHUNK), :], s16[d], preferred_element_type=F32) for d in range(2)]
        rs = [jnp.dot(wqg_ref[d, pl.ds(2 * t0s[d], 2 * CHUNK), :],
                      s16[d], preferred_element_type=F32) for d in range(2)]
        if pending is not None:
            emit_outputs(*pending)
        new_states = tuple(states[d] * egl_ref[d, pl.ds(n, 1), :] - ms[d] + sn_ref[d, pl.ds(t0s[d], CHUNK), :]
                           for d, n in enumerate((i, n_chunks - 1 - i)))
        v_new = tuple((u_ref[d, pl.ds(t0s[d], CHUNK), :] - rs[d][:CHUNK]).astype(BF16) for d in range(2))
        rq = tuple(rs[d][CHUNK:] for d in range(2))
        return new_states, (i, v_new, rq)

    zero = jnp.zeros((HEAD_W, HEAD_W), F32)
    states, pending = (zero, zero), None
    for i in range(n_chunks):
        states, pending = state_step(i, states, pending)
    emit_outputs(*pending)


def _gdn(qkv, gcol, grow, z, conv_w, gain):
    b, s, _ = qkv.shape
    nh = N_HEADS
    n_chunks = s // CHUNK
    unroll = min(MAX_LOCAL_UNROLL, n_chunks // 2)
    assert s % CHUNK == 0 and n_chunks % (2 * unroll) == 0
    seq_spec = lambda off: pl.BlockSpec((1, s, HEAD_W), lambda bi, hi: (bi, 0, hi + off))
    w_spec = lambda off: pl.BlockSpec((CONV_W, HEAD_W), lambda bi, hi: (0, hi + off))
    return pl.pallas_call(
        functools.partial(_gdn_kernel, unroll=unroll),
        grid=(b, nh),
        in_specs=[
            seq_spec(0), seq_spec(nh), seq_spec(2 * nh),
            pl.BlockSpec((1, s, LANES), lambda bi, hi: (bi, 0, 0)),
            pl.BlockSpec((2 * nh, 1, n_chunks, CHUNK), lambda bi, hi: (0, bi, 0, 0)),
            seq_spec(nh),
            w_spec(0), w_spec(nh), w_spec(2 * nh),
            pl.BlockSpec((1, HEAD_W), lambda bi, hi: (0, 0)),
        ],
        out_specs=seq_spec(0),
        out_shape=jax.ShapeDtypeStruct((b, s, nh * HEAD_W), BF16),
        scratch_shapes=[
            pltpu.VMEM((3, unroll * CHUNK, HEAD_W), F32),
            pltpu.VMEM((3, unroll * CHUNK, HEAD_W), F32),
            pltpu.VMEM((2, s, HEAD_W), F32),
            pltpu.VMEM((2, 2 * s, HEAD_W), BF16),
            pltpu.VMEM((2, s, HEAD_W), BF16),
            pltpu.VMEM((2, s, HEAD_W), F32),
            pltpu.VMEM((2, s, CHUNK), BF16),
            pltpu.VMEM((2, n_chunks, LANES), F32),
            pltpu.VMEM((2, s, HEAD_W), F32),
        ],
        compiler_params=_cparams(("parallel", "arbitrary")),
        name="gated_delta",
    )(qkv, qkv, qkv, gcol, grow, z, conv_w, conv_w, conv_w, gain.reshape(1, HEAD_W))


def _merge_kernel(ya_ref, yb_ref, gate_ref, x_ref, wa_ref, wb_ref, wo_ref, g_ref, *out_refs, final, tn):
    d = x_ref.shape[1]
    merged = []
    for j in range(d // tn):
        sl = slice(j * tn, (j + 1) * tn)
        pa = jnp.dot(ya_ref[...], wa_ref[:, sl], preferred_element_type=F32)
        pb = jnp.dot(yb_ref[...], wb_ref[:, sl], preferred_element_type=F32)
        ga = gate_ref[:, j * tn:(j + 1) * tn].astype(F32)
        gb = gate_ref[:, d + j * tn:d + (j + 1) * tn].astype(F32)
        merged.append((ga * pa + gb * pb).astype(BF16))
    merged = jnp.concatenate(merged, axis=1)
    x = x_ref[...] + jnp.dot(merged, wo_ref[...], preferred_element_type=F32)
    y = x * lax.rsqrt(jnp.mean(x * x, axis=-1, keepdims=True) + NORM_EPS) * g_ref[...]
    if final:
        out_refs[0][...] = y
    else:
        out_refs[0][...] = x
        out_refs[1][...] = y.astype(BF16)


def _merge(ya, yb, gates, x, wa, wb, wo, g_next, final, tm=512, tn=256):
    m, d = x.shape
    tm = min(tm, m)
    row = lambda width: pl.BlockSpec((tm, width), lambda i: (i, 0))
    full = lambda r, c: pl.BlockSpec((r, c), lambda i: (0, 0))
    if final:
        out_shape = [jax.ShapeDtypeStruct((m, d), F32)]
        out_specs = [row(d)]
    else:
        out_shape = [jax.ShapeDtypeStruct((m, d), F32), jax.ShapeDtypeStruct((m, d), BF16)]
        out_specs = [row(d), row(d)]
    return pl.pallas_call(
        functools.partial(_merge_kernel, final=final, tn=tn),
        grid=(m // tm,),
        in_specs=[row(d), row(d), row(2 * d), row(d), full(d, d), full(d, d), full(d, d), full(1, d)],
        out_specs=out_specs,
        out_shape=out_shape,
        compiler_params=_cparams(("parallel",)),
        name="merge_out",
    )(ya, yb, gates, x, wa, wb, wo, g_next.reshape(1, d))


def _rope_tables(s):
    inv = ROPE_THETA ** (-jnp.arange(0, HD_QK, 2, dtype=F32) / HD_QK)
    ang = jnp.arange(s, dtype=F32)[:, None] * inv[None, :]
    ang = jnp.concatenate([ang, ang], -1)
    sign = jnp.where(jnp.arange(HD_QK) < HD_QK // 2, -1.0, 1.0).astype(F32)
    cos = jnp.tile(jnp.cos(ang), (1, LANES // HD_QK))
    sin = jnp.tile(jnp.sin(ang) * sign[None, :], (1, LANES // HD_QK))
    return cos, sin


def _trunk(x, norm_g, w_in, conv_w, lam_qk, diff_norm_g, a_log, dt_bias, gdn_norm_g, w_branch, w_out, final_g):
    b, s, d = x.shape
    depth = w_in.shape[0]
    m = b * s
    w_a = N_HEADS * HEAD_W
    qk_w = N_HEADS * 2 * HD_QK
    c_q, c_k, c_v, c_za = 0, qk_w, 2 * qk_w, 2 * qk_w + w_a
    c_qkvb = c_za + w_a
    c_zb = c_qkvb + 3 * w_a
    c_a = c_zb + w_a
    c_b = c_a + 2 * N_HEADS
    c_gate = c_b + 2 * N_HEADS

    cos, sin = _rope_tables(s)
    tm = min(1024, s)
    pos_blocks = s // tm
    rope_specs = [pl.BlockSpec((tm, LANES), lambda i: (i % pos_blocks, 0))] * 2

    x2 = x.reshape(m, d)
    h = _rmsnorm(x2, norm_g[0], BF16)
    for l in range(depth):
        w = w_in[l]
        scale = HD_QK ** -0.5 * math.log2(math.e)
        w_qk = jnp.concatenate([w[:, c_q:c_k] * scale, w[:, c_k:c_v]], axis=1).astype(BF16)
        w_v = w[:, c_v:c_za].astype(BF16)
        w_z = jnp.concatenate([w[:, c_za:c_qkvb], w[:, c_zb:c_a]], axis=1).astype(BF16)
        w_qkvb = w[:, c_qkvb:c_zb].astype(BF16)
        w_ab = jnp.pad(w[:, c_a:c_gate], ((0, 0), (0, LANES - 4 * N_HEADS))).astype(BF16)
        w_gate = w[:, c_gate:].astype(BF16)
        alog_vec = jnp.pad(a_log[l].reshape(1, -1), ((0, 0), (0, LANES - 2 * N_HEADS)))
        dtb_vec = jnp.pad(dt_bias[l].reshape(1, -1), ((0, 0), (0, LANES - 2 * N_HEADS)))

        qk = _proj(_proj_rope_kernel, h, w_qk, (cos, sin), rope_specs, BF16, tm, "proj_qk_rope")
        v_a = _proj(_proj_plain_kernel, h, w_v, (), [], BF16, tm, "proj_v")
        z_ab = _proj(_proj_plain_kernel, h, w_z, (), [], BF16, tm, "proj_z")
        qkv_b = _proj(_proj_plain_kernel, h, w_qkvb, (), [], BF16, tm, "proj_qkv_delta")
        gcol, grow = _proj_gdn_gates(h, w_ab, alog_vec, dtb_vec, tm)
        merge_gates = _proj(_proj_sigmoid_kernel, h, w_gate, (), [], BF16, tm, "proj_merge_gates")

        lam_init = 0.8 - 0.6 * math.exp(-0.3 * l)
        lq = lam_qk[l].astype(F32)
        lam = (jnp.exp(jnp.sum(lq[0] * lq[1])) - jnp.exp(jnp.sum(lq[2] * lq[3])) + lam_init).reshape(1)

        z_ab = z_ab.reshape(b, s, 2 * w_a)
        y_a = _diff_attention(lam, qk.reshape(b, s, 2 * qk_w), v_a.reshape(b, s, w_a), z_ab,
                              diff_norm_g[l], 1.0 - lam_init)
        y_b = _gdn(qkv_b.reshape(b, s, 3 * w_a), gcol.reshape(b, s, LANES),
                   grow.reshape(2 * N_HEADS, b, s // CHUNK, CHUNK), z_ab, conv_w[l], gdn_norm_g[l])

        final = l == depth - 1
        g_next = final_g if final else norm_g[l + 1]
        outs = _merge(y_a.reshape(m, w_a), y_b.reshape(m, w_a), merge_gates, x2,
                      w_branch[l, 0].astype(BF16), w_branch[l, 1].astype(BF16), w_out[l].astype(BF16),
                      g_next, final)
        if final:
            x2 = outs[0]
        else:
            x2, h = outs
    return x2.reshape(b, s, d)


def kernel(x_prompt, x_sample, norm_g, w_in, conv_w, lam_qk, diff_norm_g, a_log, dt_bias, gdn_norm_g, w_branch, w_out, final_g):
    params = (norm_g, w_in, conv_w, lam_qk, diff_norm_g, a_log, dt_bias, gdn_norm_g, w_branch, w_out, final_g)
    return (_trunk(x_prompt, *params), _trunk(x_sample, *params))
```
